```python
import jax, jax.numpy as jnp
from jax import lax
import numpy as np

D_MODEL = 1024
BATCH = 2
SEQ = 8192
DEPTH = 4

GRID_W = 64
CTX_LEN = 256
HEAD_DIM = 64
ATTN_WIDTH = D_MODEL // 2
N_Q_HEADS = ATTN_WIDTH // HEAD_DIM
N_KV_HEADS = 2
Q_GROUP = N_Q_HEADS // N_KV_HEADS
KV_WIDTH = N_KV_HEADS * HEAD_DIM
ATTN_SCALE = HEAD_DIM ** -0.5
Q_BLOCK = 128
ROPE_THETA = 10000.0
POOL_WINDOWS = (2, 4, 8, 16)
POOL_WIDTH = D_MODEL - ATTN_WIDTH
POOL_GROUP_DIM = POOL_WIDTH // len(POOL_WINDOWS)
IN_A_WIDTH = ATTN_WIDTH + 2 * KV_WIDTH + POOL_WIDTH
OUT_A_WIDTH = ATTN_WIDTH + POOL_WIDTH
CHUNK = 128
GMLP_WIDTH = D_MODEL
GMLP_GROUPS = 8
GMLP_GROUP_DIM = GMLP_WIDTH // GMLP_GROUPS
D_FF = 2816
N_MOD = 9
EPS = 1e-6
N_EVEN = (DEPTH + 1) // 2
N_ODD = DEPTH // 2

kernel_name = "hybrid_attn_pool_gmlp_dit_trunk"


def rms_norm(x, g):
    xf = x.astype(jnp.float32)
    y = xf * lax.rsqrt(jnp.mean(xf * xf, axis=-1, keepdims=True) + EPS)
    return (y * g.astype(jnp.float32)).astype(x.dtype)


def modulate(x, g, shift, scale):
    return rms_norm(x, g) * (1 + scale) + shift


def swiglu(h, w13, w2):
    a, b = jnp.split(h @ w13, 2, axis=-1)
    return (jax.nn.silu(a) * b) @ w2


def half_ffn(x, g, shift, scale, gate, w13, w2):
    return x + 0.5 * gate * swiglu(modulate(x, g, shift, scale), w13, w2)


def axial_rope_tables(n_tokens, dtype):
    n_rows = n_tokens // GRID_W
    rows = jnp.repeat(jnp.arange(n_rows), GRID_W).astype(jnp.float32)
    cols = jnp.tile(jnp.arange(GRID_W), n_rows).astype(jnp.float32)
    half = HEAD_DIM // 2
    inv_freq = ROPE_THETA ** (-jnp.arange(0, half, 2, dtype=jnp.float32) / half)
    ang_r = rows[:, None] * inv_freq[None, :]
    ang_c = cols[:, None] * inv_freq[None, :]
    return (jnp.cos(ang_r).astype(dtype), jnp.sin(ang_r).astype(dtype),
            jnp.cos(ang_c).astype(dtype), jnp.sin(ang_c).astype(dtype))


def rope_half(x, cos, sin):
    x1, x2 = jnp.split(x, 2, axis=-1)
    cos = cos[None, :, None, :]
    sin = sin[None, :, None, :]
    return jnp.concatenate([x1 * cos - x2 * sin, x1 * sin + x2 * cos], axis=-1)


def apply_axial_rope(x, tabs):
    cr, sr, cc, sc = tabs
    xr, xc = jnp.split(x, 2, axis=-1)
    return jnp.concatenate([rope_half(xr, cr, sr), rope_half(xc, cc, sc)], axis=-1)


def gqa_attend(q, k, v):
    b, nq = q.shape[:2]
    qg = q.reshape(b, nq, N_KV_HEADS, Q_GROUP, HEAD_DIM)
    sc = jnp.einsum('bqhgd,bkhd->bhgqk', qg, k, preferred_element_type=jnp.float32) * ATTN_SCALE
    pr = jax.nn.softmax(sc, axis=-1).astype(v.dtype)
    return jnp.einsum('bhgqk,bkhd->bqhgd', pr, v).reshape(b, nq, ATTN_WIDTH)


def latent_attention(q, k_lat, v_lat, k_ctx, v_ctx):
    b, s = q.shape[:2]
    k_all = jnp.concatenate([k_ctx, k_lat], axis=1)
    v_all = jnp.concatenate([v_ctx, v_lat], axis=1)
    nb = s // Q_BLOCK
    qb = jnp.moveaxis(q.reshape(b, nb, Q_BLOCK, N_Q_HEADS, HEAD_DIM), 1, 0)
    ob = lax.map(lambda qblk: gqa_attend(qblk, k_all, v_all), qb)
    return jnp.moveaxis(ob, 0, 1).reshape(b, s, ATTN_WIDTH)


def project_a(h, w_in, qk_g):
    b, n = h.shape[:2]
    z = h @ w_in
    q = rms_norm(z[..., :ATTN_WIDTH].reshape(b, n, N_Q_HEADS, HEAD_DIM), qk_g[0])
    k = rms_norm(z[..., ATTN_WIDTH:ATTN_WIDTH + KV_WIDTH].reshape(b, n, N_KV_HEADS, HEAD_DIM), qk_g[1])
    v = z[..., ATTN_WIDTH + KV_WIDTH:ATTN_WIDTH + 2 * KV_WIDTH].reshape(b, n, N_KV_HEADS, HEAD_DIM)
    p = z[..., ATTN_WIDTH + 2 * KV_WIDTH:]
    return q, k, v, p


def project_ctx_kv(h, w_in, qk_g):
    b, n = h.shape[:2]
    z = h @ w_in[:, ATTN_WIDTH:ATTN_WIDTH + 2 * KV_WIDTH]
    k = rms_norm(z[..., :KV_WIDTH].reshape(b, n, N_KV_HEADS, HEAD_DIM), qk_g[1])
    v = z[..., KV_WIDTH:].reshape(b, n, N_KV_HEADS, HEAD_DIM)
    return k, v


def multiscale_pool(p, pool_w, pool_scale):
    n = p.shape[1]
    pf = p.astype(jnp.float32)
    cs = jnp.concatenate([jnp.zeros_like(pf[:, :1]), jnp.cumsum(pf, axis=1)], axis=1)
    t = jnp.arange(n)
    outs = []
    for gi, w in enumerate(POOL_WINDOWS):
        sl = slice(gi * POOL_GROUP_DIM, (gi + 1) * POOL_GROUP_DIM)
        lo = jnp.clip(t - w // 2, 0, n)
        hi = jnp.clip(t + w // 2, 0, n)
        csg = cs[..., sl]
        mean = (csg[:, hi] - csg[:, lo]) / (hi - lo).astype(jnp.float32)[None, :, None]
        outs.append((mean - pf[..., sl]).astype(p.dtype) @ pool_w[gi])
    return jnp.concatenate(outs, axis=-1) * pool_scale


def combine_a(attn, p, pool_w, pool_scale, w_out):
    return jnp.concatenate([attn, multiscale_pool(p, pool_w, pool_scale)], axis=-1) @ w_out


def mixer_c(h, w_in, v_g, w_sp, b_sp, w_out):
    b, n = h.shape[:2]
    z = jax.nn.gelu(h @ w_in)
    u, v = jnp.split(z, 2, axis=-1)
    v = rms_norm(v, v_g)
    vc = v.reshape(b, n // CHUNK, CHUNK, GMLP_GROUPS, GMLP_GROUP_DIM)
    sv = jnp.einsum('gpq,bnqgc->bnpgc', w_sp, vc) + b_sp.T[None, None, :, :, None]
    return (u * sv.reshape(b, n, GMLP_WIDTH)) @ w_out


def setup_inputs(seed: int = 0) -> dict:
    key = jax.random.key(seed)
    ks = jax.random.split(key, 20)
    f32 = jnp.float32

    def nrm(k, shape, fan_in):
        return jax.random.normal(k, shape, f32) * (fan_in ** -0.5)

    def gain(k, shape):
        return 1.0 + 0.05 * jax.random.normal(k, shape, f32)

    return {
        "x": jax.random.normal(ks[0], (BATCH, SEQ, D_MODEL), f32),
        "c": jax.random.normal(ks[1], (BATCH, D_MODEL), f32),
        "ctx": jax.random.normal(ks[2], (BATCH, CTX_LEN, D_MODEL), f32),
        "c_ctx": jax.random.normal(ks[3], (D_MODEL,), f32),
        "w_mod": 0.5 * nrm(ks[4], (DEPTH, D_MODEL, N_MOD * D_MODEL), D_MODEL),
        "b_mod": 0.01 * jax.random.normal(ks[5], (DEPTH, N_MOD * D_MODEL), f32),
        "norm_g": gain(ks[6], (DEPTH, 3, D_MODEL)),
        "ffn_w13": nrm(ks[7], (DEPTH, 2, D_MODEL, 2 * D_FF), D_MODEL),
        "ffn_w2": nrm(ks[8], (DEPTH, 2, D_FF, D_MODEL), D_FF),
        "w_in_a": nrm(ks[9], (N_EVEN, D_MODEL, IN_A_WIDTH), D_MODEL),
        "qk_norm_g": gain(ks[10], (N_EVEN, 2, HEAD_DIM)),
        "pool_w": nrm(ks[11], (N_EVEN, len(POOL_WINDOWS), POOL_GROUP_DIM, POOL_GROUP_DIM), POOL_GROUP_DIM),
        "pool_scale": gain(ks[12], (N_EVEN, POOL_WIDTH)),
        "w_out_a": nrm(ks[13], (N_EVEN, OUT_A_WIDTH, D_MODEL), OUT_A_WIDTH),
        "w_in_c": nrm(ks[14], (N_ODD, D_MODEL, 2 * GMLP_WIDTH), D_MODEL),
        "v_norm_g": gain(ks[15], (N_ODD, GMLP_WIDTH)),
        "w_sp": nrm(ks[16], (N_ODD, GMLP_GROUPS, CHUNK, CHUNK), CHUNK),
        "b_sp": gain(ks[17], (N_ODD, GMLP_GROUPS, CHUNK)),
        "w_out_c": nrm(ks[18], (N_ODD, GMLP_WIDTH, D_MODEL), GMLP_WIDTH),
    }


def reference(x, c, ctx, c_ctx, w_mod, b_mod, norm_g, ffn_w13, ffn_w2,
              w_in_a, qk_norm_g, pool_w, pool_scale, w_out_a,
              w_in_c, v_norm_g, w_sp, b_sp, w_out_c):
    b, s, d = x.shape
    rope_tabs = axial_rope_tables(s, x.dtype)
    silu_c = jax.nn.silu(c)
    silu_cc = jax.nn.silu(c_ctx)
    cs = ctx
    for i in range(DEPTH):
        m = (silu_c @ w_mod[i] + b_mod[i]).reshape(b, 1, N_MOD, d)
        reads_ctx = (i % 2 == 0)
        later_reads_ctx = any(j % 2 == 0 for j in range(i + 1, DEPTH))
        ctx_needed = reads_ctx or later_reads_ctx
        if ctx_needed:
            mc = (silu_cc @ w_mod[i] + b_mod[i]).reshape(1, 1, N_MOD, d)
            cs = half_ffn(cs, norm_g[i, 0], mc[:, :, 0], mc[:, :, 1], mc[:, :, 2],
                          ffn_w13[i, 0], ffn_w2[i, 0])
        x = half_ffn(x, norm_g[i, 0], m[:, :, 0], m[:, :, 1], m[:, :, 2], ffn_w13[i, 0], ffn_w2[i, 0])

        h = modulate(x, norm_g[i, 1], m[:, :, 3], m[:, :, 4])
        if i % 2 == 0:
            e = i // 2
            hc = modulate(cs, norm_g[i, 1], mc[:, :, 3], mc[:, :, 4])
            if later_reads_ctx:
                qc, kc, vc, pc = project_a(hc, w_in_a[e], qk_norm_g[e])
            else:
                kc, vc = project_ctx_kv(hc, w_in_a[e], qk_norm_g[e])
            q, k, v, p = project_a(h, w_in_a[e], qk_norm_g[e])
            q = apply_axial_rope(q, rope_tabs)
            k = apply_axial_rope(k, rope_tabs)
            attn = latent_attention(q, k, v, kc, vc)
            x = x + m[:, :, 5] * combine_a(attn, p, pool_w[e], pool_scale[e], w_out_a[e])
            if later_reads_ctx:
                attn_c = gqa_attend(qc, kc, vc)
                cs = cs + mc[:, :, 5] * combine_a(attn_c, pc, pool_w[e], pool_scale[e], w_out_a[e])
        else:
            o = i // 2
            x = x + m[:, :, 5] * mixer_c(h, w_in_c[o], v_norm_g[o], w_sp[o], b_sp[o], w_out_c[o])
            if later_reads_ctx:
                hc = modulate(cs, norm_g[i, 1], mc[:, :, 3], mc[:, :, 4])
                cs = cs + mc[:, :, 5] * mixer_c(hc, w_in_c[o], v_norm_g[o], w_sp[o], b_sp[o], w_out_c[o])

        if later_reads_ctx:
            cs = half_ffn(cs, norm_g[i, 2], mc[:, :, 6], mc[:, :, 7], mc[:, :, 8],
                          ffn_w13[i, 1], ffn_w2[i, 1])
        x = half_ffn(x, norm_g[i, 2], m[:, :, 6], m[:, :, 7], m[:, :, 8], ffn_w13[i, 1], ffn_w2[i, 1])
    return x
```

```python
import functools

import jax
import jax.numpy as jnp
import numpy as np
from jax import lax
from jax.experimental import pallas as pl
from jax.experimental.pallas import tpu as pltpu

F32 = jnp.float32
BF16 = jnp.bfloat16

GRID_W = 64
HEAD_DIM = 64
N_Q_HEADS = 8
N_KV_HEADS = 2
Q_GROUP = N_Q_HEADS // N_KV_HEADS
ATTN_WIDTH = N_Q_HEADS * HEAD_DIM
KV_WIDTH = N_KV_HEADS * HEAD_DIM
ATTN_SCALE = HEAD_DIM ** -0.5
ROPE_THETA = 10000.0
POOL_WINDOWS = (2, 4, 8, 16)
POOL_GROUP_DIM = 128
CHUNK = 128
GMLP_GROUPS = 8
N_MOD = 9
EPS = 1e-6

LANES = 128
SUBLANES = 8
VMEM_LIMIT_BYTES = 56 * 1024 * 1024

TM = 512
FF_CHUNK = 256
TQ = 128
TK = 512
HALO = 8


def _cparams(n_axes):
    return pltpu.CompilerParams(
        dimension_semantics=("arbitrary",) * n_axes,
        vmem_limit_bytes=VMEM_LIMIT_BYTES)


def _resident(shape):
    zeros = (0,) * len(shape)
    return pl.BlockSpec(shape, lambda *_: zeros, pipeline_mode=pl.Buffered(1))


def _modulate(x, g, shift, scale):
    ms = jnp.mean(x * x, axis=-1, keepdims=True)
    return (x * lax.rsqrt(ms + EPS) * g) * (1.0 + scale) + shift


def _mod_kernel(c_ref, w_ref, b_ref, o_ref):
    c = c_ref[...]
    s = (c * jax.nn.sigmoid(c)).astype(BF16)
    o_ref[0, 0] = jnp.dot(s, w_ref[0].astype(BF16), preferred_element_type=F32) + b_ref[0, 0]


def _modulation(c_rows, w_mod, b_mod):
    depth, d, _ = w_mod.shape
    rows = c_rows.shape[0]
    out = pl.pallas_call(
        _mod_kernel,
        grid=(depth, N_MOD),
        in_specs=[
            pl.BlockSpec((rows, d), lambda i, j: (0, 0)),
            pl.BlockSpec((1, d, d), lambda i, j: (i, 0, j)),
            pl.BlockSpec((1, 1, 1, d), lambda i, j: (i, j, 0, 0)),
        ],
        out_specs=pl.BlockSpec((1, 1, rows, d), lambda i, j: (i, j, 0, 0)),
        out_shape=jax.ShapeDtypeStruct((depth, N_MOD, rows, d), F32),
        compiler_params=_cparams(2),
        name="adaln_modulation",
    )(c_rows, w_mod, b_mod.reshape(depth, N_MOD, 1, d))
    return out


def _ffn_kernel(x_ref, mod_ref, g_ref, wa_ref, wb_ref, w2_ref, o_ref, h_ref, acc_ref, *, k0):
    x = x_ref[...]
    mod = mod_ref[0]
    h_ref[...] = _modulate(x, g_ref[...], mod[k0:k0 + 1], mod[k0 + 1:k0 + 2]).astype(BF16)
    acc_ref[...] = jnp.zeros_like(acc_ref)

    def body(c, carry):
        h = h_ref[...]
        a = jnp.dot(h, wa_ref[c], preferred_element_type=F32)
        b = jnp.dot(h, wb_ref[c], preferred_element_type=F32)
        act = (a * jax.nn.sigmoid(a) * b).astype(BF16)
        acc_ref[...] += jnp.dot(act, w2_ref[c], preferred_element_type=F32)
        return carry

    lax.fori_loop(0, wa_ref.shape[0], body, 0)
    o_ref[...] = x + 0.5 * mod[k0 + 2:k0 + 3] * acc_ref[...]


def _half_ffn(xs, mods, g, wa, wb, w2, *, k0, n_tiles, group_of_tile):
    d = xs.shape[1]
    return pl.pallas_call(
        functools.partial(_ffn_kernel, k0=k0),
        grid=(n_tiles,),
        in_specs=[
            pl.BlockSpec((TM, d), lambda t: (t, 0)),
            pl.BlockSpec((1, N_MOD, d), lambda t: (group_of_tile(t), 0, 0)),
            pl.BlockSpec((1, d), lambda t: (0, 0)),
            _resident(wa.shape),
            _resident(wb.shape),
            _resident(w2.shape),
        ],
        out_specs=pl.BlockSpec((TM, d), lambda t: (t, 0)),
        out_shape=jax.ShapeDtypeStruct((n_tiles * TM, d), F32),
        scratch_shapes=[pltpu.VMEM((TM, d), BF16), pltpu.VMEM((TM, d), F32)],
        compiler_params=_cparams(1),
        name="half_ffn",
    )(xs, mods, g, wa, wb, w2)


def _rope(x, cos, sin_hi, sin_lo):
    return (x * cos + pltpu.roll(x, 16, axis=1) * sin_hi
            + pltpu.roll(x, LANES - 16, axis=1) * sin_lo)


def _proj_a_kernel(x_ref, mod_ref, g_ref, w_ref, gq_ref, gk_ref, hsum_ref,
                   cos_ref, shi_ref, slo_ref, q_ref, k_ref, v_ref, p_ref):
    mod = mod_ref[0]
    h = _modulate(x_ref[...], g_ref[...], mod[3:4], mod[4:5]).astype(BF16)
    z = jnp.dot(h, w_ref[...], preferred_element_type=F32)
    cos, shi, slo = cos_ref[...], shi_ref[...], slo_ref[...]
    hsum = hsum_ref[...]

    def norm_rope(zb, gain):
        ms = jnp.dot((zb * zb).astype(BF16), hsum, preferred_element_type=F32)
        return _rope(zb * lax.rsqrt(ms + EPS) * gain, cos, shi, slo)

    for j in range(ATTN_WIDTH // LANES):
        qb = norm_rope(z[:, j * LANES:(j + 1) * LANES], gq_ref[...])
        q_ref[:, j * LANES:(j + 1) * LANES] = (qb * ATTN_SCALE).astype(BF16)
    k_ref[...] = norm_rope(z[:, ATTN_WIDTH:ATTN_WIDTH + KV_WIDTH], gk_ref[...]).astype(BF16)
    v_ref[...] = z[:, ATTN_WIDTH + KV_WIDTH:ATTN_WIDTH + 2 * KV_WIDTH].astype(BF16)
    p_ref[...] = z[:, ATTN_WIDTH + 2 * KV_WIDTH:]


def _project_a(xs, mods, g, w_in, gq, gk, hsum, cos, shi, slo, *, n_tiles, group_of_tile):
    d = xs.shape[1]
    nt = n_tiles * TM
    pool_width = w_in.shape[1] - ATTN_WIDTH - 2 * KV_WIDTH
    row_tile = lambda w: pl.BlockSpec((TM, w), lambda t: (t, 0))
    const2 = lambda shape: pl.BlockSpec(shape, lambda t: (0, 0))
    return pl.pallas_call(
        _proj_a_kernel,
        grid=(n_tiles,),
        in_specs=[
            row_tile(d),
            pl.BlockSpec((1, N_MOD, d), lambda t: (group_of_tile(t), 0, 0)),
            const2((1, d)),
            _resident(w_in.shape),
            const2((1, LANES)), const2((1, LANES)), const2((LANES, LANES)),
            row_tile(LANES), row_tile(LANES), row_tile(LANES),
        ],
        out_specs=[row_tile(ATTN_WIDTH), row_tile(KV_WIDTH), row_tile(KV_WIDTH), row_tile(pool_width)],
        out_shape=[
            jax.ShapeDtypeStruct((nt, ATTN_WIDTH), BF16),
            jax.ShapeDtypeStruct((nt, KV_WIDTH), BF16),
            jax.ShapeDtypeStruct((nt, KV_WIDTH), BF16),
            jax.ShapeDtypeStruct((nt, pool_width), F32),
        ],
        compiler_params=_cparams(1),
        name="mixer_a_project",
    )(xs, mods, g, w_in, gq, gk, hsum, cos, shi, slo)


def _attn_kernel(*refs, n_lat_steps, has_ctx):
    if has_ctx and n_lat_steps:
        q_ref, kc_ref, vc_ref, kl_ref, vl_ref, o_ref, m_ref, l_ref, acc_ref = refs
    elif has_ctx:
        q_ref, kc_ref, vc_ref, o_ref, m_ref, l_ref, acc_ref = refs
    tq = q_ref.shape[0]
    n_blocks = ATTN_WIDTH // LANES
    left = lax.broadcasted_iota(jnp.int32, (tq, LANES), 1) < HEAD_DIM
    pieces = []
    for j in range(n_blocks):
        blk = q_ref[:, j * LANES:(j + 1) * LANES]
        zero = jnp.zeros_like(blk)
        pieces += [jnp.where(left, blk, zero), jnp.where(left, zero, blk)]
    qq = jnp.concatenate(pieces, axis=0)

    m_ref[...] = jnp.full_like(m_ref, -jnp.inf)
    l_ref[...] = jnp.zeros_like(l_ref)
    acc_ref[...] = jnp.zeros_like(acc_ref)

    def step(k, v):
        s = lax.dot_general(qq, k, (((1,), (1,)), ((), ())), preferred_element_type=F32)
        m_old = m_ref[...]
        m_new = jnp.maximum(m_old, jnp.max(s, axis=1, keepdims=True))
        alpha = jnp.exp(m_old - m_new)
        p = jnp.exp(s - m_new)
        l_ref[...] = alpha * l_ref[...] + jnp.sum(p, axis=1, keepdims=True)
        acc_ref[...] = alpha * acc_ref[...] + jnp.dot(p.astype(BF16), v, preferred_element_type=F32)
        m_ref[...] = m_new

    step(kc_ref[...], vc_ref[...])
    if n_lat_steps:
        def body(i, carry):
            rows = pl.ds(pl.multiple_of(i * TK, TK), TK)
            step(kl_ref[rows, :], vl_ref[rows, :])
            return carry
        lax.fori_loop(0, n_lat_steps, body, 0)

    out = acc_ref[...] / l_ref[...]
    for j in range(n_blocks):
        o_left = out[(2 * j) * tq:(2 * j + 1) * tq]
        o_right = out[(2 * j + 1) * tq:(2 * j + 2) * tq]
        o_ref[:, j * LANES:(j + 1) * LANES] = jnp.where(left, o_left, o_right).astype(o_ref.dtype)


def _attention(q, k, v, *, batch, seq, ctx_len, latent):
    lat_rows = batch * seq
    ctx_blk0 = lat_rows // ctx_len
    n_q = (seq if latent else ctx_len) // TQ
    q_blk0 = 0 if latent else lat_rows // TQ
    in_specs = [
        pl.BlockSpec((TQ, ATTN_WIDTH), lambda b, i: (q_blk0 + b * n_q + i, 0)),
        pl.BlockSpec((ctx_len, KV_WIDTH), lambda b, i: (ctx_blk0 + b, 0)),
        pl.BlockSpec((ctx_len, KV_WIDTH), lambda b, i: (ctx_blk0 + b, 0)),
    ]
    operands = [q, k, v]
    if latent:
        in_specs += [pl.BlockSpec((seq, KV_WIDTH), lambda b, i: (b, 0))] * 2
        operands += [k, v]
    rows = N_Q_HEADS * TQ
    return pl.pallas_call(
        functools.partial(_attn_kernel, n_lat_steps=seq // TK if latent else 0, has_ctx=True),
        grid=(batch, n_q),
        in_specs=in_specs,
        out_specs=pl.BlockSpec((TQ, ATTN_WIDTH), lambda b, i: (b * n_q + i, 0)),
        out_shape=jax.ShapeDtypeStruct((batch * n_q * TQ, ATTN_WIDTH), BF16),
        scratch_shapes=[pltpu.VMEM((rows, 1), F32), pltpu.VMEM((rows, 1), F32),
                        pltpu.VMEM((rows, KV_WIDTH), F32)],
        compiler_params=_cparams(2),
        name="gqa_attention_latent" if latent else "gqa_attention_ctx",
    )(*operands)


def _shift_rows(a, d):
    return pltpu.roll(a, d % a.shape[0], axis=0)


def _pool_group(pe, pos, n, w):
    zero = jnp.zeros_like(pe)
    left = jnp.where(pos >= 1, _shift_rows(pe, 1), zero)
    right = pe
    span = 1
    while 2 * span <= w // 2:
        left = left + jnp.where(pos >= span, _shift_rows(left, span), zero)
        right = right + jnp.where(pos + span < n, _shift_rows(right, -span), zero)
        span *= 2
    count = jnp.maximum(jnp.minimum(pos + w // 2, n) - jnp.maximum(pos - w // 2, 0), 1)
    return (left + right) / count.astype(F32)


def _rem_static(r, m):
    return r & (m - 1) if m & (m - 1) == 0 else lax.rem(r, m)


def _comb_a_kernel(x_ref, mod_ref, attn_ref, p_ref, pprev_ref, pnext_ref, pw_ref, ps_ref,
                   wo_ref, o_ref, cat_ref, *, lat_rows, seq, ctx_len):
    rows = TM + 2 * HALO
    row0 = pl.program_id(0) * TM - HALO
    r = row0 + lax.broadcasted_iota(jnp.int32, (rows, POOL_GROUP_DIM), 0)
    is_lat = r < lat_rows
    n = jnp.where(is_lat, seq, ctx_len)
    pos = jnp.where(is_lat, _rem_static(r, seq), _rem_static(r - lat_rows, ctx_len))

    cat_ref[:, :ATTN_WIDTH] = attn_ref[...]
    for gi, w in enumerate(POOL_WINDOWS):
        cols = slice(gi * POOL_GROUP_DIM, (gi + 1) * POOL_GROUP_DIM)
        pe = jnp.concatenate([pprev_ref[:, cols], p_ref[:, cols], pnext_ref[:, cols]], axis=0)
        centred = (_pool_group(pe, pos, n, w) - pe)[HALO:HALO + TM]
        pooled = jnp.dot(centred.astype(BF16), pw_ref[gi], preferred_element_type=F32) * ps_ref[:, cols]
        cat_ref[:, ATTN_WIDTH + gi * POOL_GROUP_DIM:ATTN_WIDTH + (gi + 1) * POOL_GROUP_DIM] = pooled.astype(BF16)
    mixed = jnp.dot(cat_ref[...], wo_ref[...], preferred_element_type=F32)
    o_ref[...] = x_ref[...] + mod_ref[0][5:6] * mixed


def _combine_a(xs, mods, attn, p, pool_w, pool_scale, w_out, *, n_tiles, group_of_tile,
               lat_rows, seq, ctx_len):
    d = xs.shape[1]
    pool_width = p.shape[1]
    halo_per_tile = TM // HALO
    last_halo_blk = p.shape[0] // HALO - 1
    return pl.pallas_call(
        functools.partial(_comb_a_kernel, lat_rows=lat_rows, seq=seq, ctx_len=ctx_len),
        grid=(n_tiles,),
        in_specs=[
            pl.BlockSpec((TM, d), lambda t: (t, 0)),
            pl.BlockSpec((1, N_MOD, d), lambda t: (group_of_tile(t), 0, 0)),
            pl.BlockSpec((TM, ATTN_WIDTH), lambda t: (t, 0)),
            pl.BlockSpec((TM, pool_width), lambda t: (t, 0)),
            pl.BlockSpec((HALO, pool_width), lambda t: (jnp.maximum(t * halo_per_tile - 1, 0), 0)),
            pl.BlockSpec((HALO, pool_width),
                         lambda t: (jnp.minimum((t + 1) * halo_per_tile, last_halo_blk), 0)),
            _resident(pool_w.shape),
            pl.BlockSpec((1, pool_width), lambda t: (0, 0)),
            _resident(w_out.shape),
        ],
        out_specs=pl.BlockSpec((TM, d), lambda t: (t, 0)),
        out_shape=jax.ShapeDtypeStruct((n_tiles * TM, d), F32),
        scratch_shapes=[pltpu.VMEM((TM, w_out.shape[0]), BF16)],
        compiler_params=_cparams(1),
        name="mixer_a_combine",
    )(xs, mods, attn, p, p, p, pool_w, pool_scale, w_out)


def _gelu_tanh(x):
    return 0.5 * x * (1.0 + jnp.tanh(np.sqrt(2.0 / np.pi).astype(np.float32) * (x + 0.044715 * (x * x * x))))


def _mixer_c_kernel(x_ref, mod_ref, g_ref, wi_ref, vg_ref, wsp_ref, bsp_ref, wo_ref, o_ref, gated_ref):
    x = x_ref[...]
    mod = mod_ref[0]
    width = wo_ref.shape[0]
    h = _modulate(x, g_ref[...], mod[3:4], mod[4:5]).astype(BF16)
    z = _gelu_tanh(jnp.dot(h, wi_ref[...], preferred_element_type=F32))
    u = z[:, :width]
    v = z[:, width:]
    v = (v * lax.rsqrt(jnp.mean(v * v, axis=-1, keepdims=True) + EPS) * vg_ref[...]).astype(BF16)
    bias = bsp_ref[...]
    for g in range(GMLP_GROUPS):
        cols = slice(g * CHUNK, (g + 1) * CHUNK)
        b_col = bias[:, g:g + 1]
        for c in range(TM // CHUNK):
            rows = slice(c * CHUNK, (c + 1) * CHUNK)
            sv = jnp.dot(wsp_ref[g], v[rows, cols], preferred_element_type=F32) + b_col
            gated_ref[rows, cols] = (u[rows, cols] * sv).astype(BF16)
    o_ref[...] = x + mod[5:6] * jnp.dot(gated_ref[...], wo_ref[...], preferred_element_type=F32)


def _mixer_c(xs, mods, g, w_in, v_g, w_sp, b_sp_cols, w_out, *, n_tiles, group_of_tile):
    d = xs.shape[1]
    width = w_out.shape[0]
    return pl.pallas_call(
        _mixer_c_kernel,
        grid=(n_tiles,),
        in_specs=[
            pl.BlockSpec((TM, d), lambda t: (t, 0)),
            pl.BlockSpec((1, N_MOD, d), lambda t: (group_of_tile(t), 0, 0)),
            pl.BlockSpec((1, d), lambda t: (0, 0)),
            _resident(w_in.shape),
            pl.BlockSpec((1, width), lambda t: (0, 0)),
            _resident(w_sp.shape),
            pl.BlockSpec(b_sp_cols.shape, lambda t: (0, 0)),
            _resident(w_out.shape),
        ],
        out_specs=pl.BlockSpec((TM, d), lambda t: (t, 0)),
        out_shape=jax.ShapeDtypeStruct((n_tiles * TM, d), F32),
        scratch_shapes=[pltpu.VMEM((TM, width), BF16)],
        compiler_params=_cparams(1),
        name="mixer_c_gmlp",
    )(xs, mods, g, w_in, v_g, w_sp, b_sp_cols, w_out)


def _rope_tables(batch, seq, n_ctx_rows):
    half = HEAD_DIM // 2
    pos = jnp.arange(seq)
    rows = (pos // GRID_W).astype(F32)
    cols = (pos % GRID_W).astype(F32)
    inv_freq = ROPE_THETA ** (-jnp.arange(0, half, 2, dtype=F32) / half)
    ang_r = rows[:, None] * inv_freq[None, :]
    ang_c = cols[:, None] * inv_freq[None, :]
    zeros = jnp.zeros_like(ang_r)
    cos = jnp.concatenate([jnp.cos(ang_r)] * 2 + [jnp.cos(ang_c)] * 2, axis=1)
    s_r, s_c = jnp.sin(ang_r), jnp.sin(ang_c)
    sin_hi = jnp.concatenate([zeros, s_r, zeros, s_c], axis=1)
    sin_lo = jnp.concatenate([-s_r, zeros, -s_c, zeros], axis=1)

    def stream(tab, ctx_value):
        lat = jnp.tile(tab, (batch, LANES // HEAD_DIM))
        return jnp.concatenate([lat, jnp.full((n_ctx_rows, LANES), ctx_value, F32)], axis=0)

    return stream(cos, 1.0), stream(sin_hi, 0.0), stream(sin_lo, 0.0)


def _head_block_order():
    order = []
    for j in range(Q_GROUP):
        order += [j, j + Q_GROUP]
    return np.concatenate([np.arange(h * HEAD_DIM, (h + 1) * HEAD_DIM) for h in order])


def kernel(x, c, ctx, c_ctx, w_mod, b_mod, norm_g, ffn_w13, ffn_w2, w_in_a, qk_norm_g, pool_w,
           pool_scale, w_out_a, w_in_c, v_norm_g, w_sp, b_sp, w_out_c):
    batch, seq, d = x.shape
    ctx_len = ctx.shape[1]
    depth = w_mod.shape[0]
    d_ff = ffn_w2.shape[2]
    lat_rows = batch * seq
    ctx_rows = batch * ctx_len
    assert seq % TM == 0 and ctx_rows % TM == 0 and seq % TK == 0 and ctx_len % TQ == 0
    assert ctx_len % HALO == 0 and lat_rows % ctx_len == 0 and d_ff % FF_CHUNK == 0
    lat_tiles = lat_rows // TM
    all_tiles = lat_tiles + ctx_rows // TM
    tiles_per_batch = seq // TM
    group_of_tile = lambda t: jnp.minimum(t // tiles_per_batch, batch)

    c_rows = jnp.concatenate([c, c_ctx[None, :], jnp.zeros((SUBLANES - batch - 1, d), F32)], axis=0)
    mods_all = _modulation(c_rows, w_mod, b_mod)
    mods_all = jnp.transpose(mods_all[:, :, :batch + 1], (0, 2, 1, 3))

    n_ff = d_ff // FF_CHUNK

    def ffn_weights(i, j):
        w13 = ffn_w13[i, j].astype(BF16)
        wa = jnp.transpose(w13[:, :d_ff].reshape(d, n_ff, FF_CHUNK), (1, 0, 2))
        wb = jnp.transpose(w13[:, d_ff:].reshape(d, n_ff, FF_CHUNK), (1, 0, 2))
        return wa, wb, ffn_w2[i, j].astype(BF16).reshape(n_ff, FF_CHUNK, d)

    head_cols = _head_block_order()
    cos, sin_hi, sin_lo = _rope_tables(batch, seq, ctx_rows)
    head_mean = jnp.asarray(np.kron(np.eye(LANES // HEAD_DIM), np.full((HEAD_DIM, HEAD_DIM), 1.0 / HEAD_DIM)), BF16)

    xs = jnp.concatenate([x.reshape(lat_rows, d), ctx.reshape(ctx_rows, d)], axis=0)
    for i in range(depth):
        mods = mods_all[i]
        reads_ctx = i % 2 == 0
        later_reads_ctx = any(j % 2 == 0 for j in range(i + 1, depth))
        n1 = all_tiles if (reads_ctx or later_reads_ctx) else lat_tiles
        n2 = all_tiles if later_reads_ctx else lat_tiles
        common = dict(group_of_tile=group_of_tile)

        xs = _half_ffn(xs, mods, norm_g[i, 0][None], *ffn_weights(i, 0), k0=0, n_tiles=n1, **common)

        if reads_ctx:
            e = i // 2
            w_in = w_in_a[e]
            w_in = jnp.concatenate([w_in[:, :ATTN_WIDTH][:, head_cols], w_in[:, ATTN_WIDTH:]], axis=1).astype(BF16)
            w_out = jnp.concatenate([w_out_a[e][:ATTN_WIDTH][head_cols], w_out_a[e][ATTN_WIDTH:]], axis=0).astype(BF16)
            gq = jnp.tile(qk_norm_g[e, 0], LANES // HEAD_DIM)[None]
            gk = jnp.tile(qk_norm_g[e, 1], LANES // HEAD_DIM)[None]
            q, k, v, p = _project_a(xs, mods, norm_g[i, 1][None], w_in, gq, gk, head_mean,
                                    cos, sin_hi, sin_lo, n_tiles=n1, **common)
            attn = _attention(q, k, v, batch=batch, seq=seq, ctx_len=ctx_len, latent=True)
            if later_reads_ctx:
                attn_c = _attention(q, k, v, batch=batch, seq=seq, ctx_len=ctx_len, latent=False)
                attn = jnp.concatenate([attn, attn_c], axis=0)
            xs = _combine_a(xs, mods, attn, p, pool_w[e].astype(BF16), pool_scale[e][None], w_out,
                            n_tiles=n2, lat_rows=lat_rows, seq=seq, ctx_len=ctx_len, **common)
        else:
            o = i // 2
            xs = _mixer_c(xs, mods, norm_g[i, 1][None], w_in_c[o].astype(BF16), v_norm_g[o][None],
                          w_sp[o].astype(BF16), b_sp[o].T, w_out_c[o].astype(BF16), n_tiles=n2, **common)

        xs = _half_ffn(xs, mods, norm_g[i, 2][None], *ffn_weights(i, 1), k0=6, n_tiles=n2, **common)
    return xs[:lat_rows].reshape(batch, seq, d)
```

```python
import functools

import jax
import jax.numpy as jnp
import numpy as np
from jax import lax
from jax.experimental import pallas as pl
from jax.experimental.pallas import tpu as pltpu

F32 = jnp.float32
BF16 = jnp.bfloat16

GRID_W = 64
HEAD_DIM = 64
N_Q_HEADS = 8
N_KV_HEADS = 2
Q_GROUP = N_Q_HEADS // N_KV_HEADS
ATTN_WIDTH = N_Q_HEADS * HEAD_DIM
KV_WIDTH = N_KV_HEADS * HEAD_DIM
ATTN_SCALE = HEAD_DIM ** -0.5
Q_SCALE_LOG2 = float(ATTN_SCALE * np.log2(np.e))
ROPE_THETA = 10000.0
POOL_WINDOWS = (2, 4, 8, 16)
POOL_GROUP_DIM = 128
CHUNK = 128
GMLP_GROUPS = 8
N_MOD = 9
EPS = 1e-6

LANES = 128
SUBLANES = 8
VMEM_LIMIT_BYTES = 56 * 1024 * 1024

TM = 512
FF_CHUNK = 256
TQ = 128
TK = 512
HALO = 8


def _cparams(n_axes):
    return pltpu.CompilerParams(
        dimension_semantics=("arbitrary",) * n_axes,
        vmem_limit_bytes=VMEM_LIMIT_BYTES)


def _resident(shape):
    zeros = (0,) * len(shape)
    return pl.BlockSpec(shape, lambda *_: zeros, pipeline_mode=pl.Buffered(1))


def _modulate(x, g, shift, scale):
    ms = jnp.mean(x * x, axis=-1, keepdims=True)
    return (x * lax.rsqrt(ms + EPS) * g) * (1.0 + scale) + shift


def _mod_kernel(c_ref, w_ref, b_ref, o_ref):
    c = c_ref[...]
    s = (c * jax.nn.sigmoid(c)).astype(BF16)
    o_ref[0, 0] = jnp.dot(s, w_ref[0].astype(BF16), preferred_element_type=F32) + b_ref[0, 0]


def _modulation(c_rows, w_mod, b_mod):
    depth, d, _ = w_mod.shape
    rows = c_rows.shape[0]
    out = pl.pallas_call(
        _mod_kernel,
        grid=(depth, N_MOD),
        in_specs=[
            pl.BlockSpec((rows, d), lambda i, j: (0, 0)),
            pl.BlockSpec((1, d, d), lambda i, j: (i, 0, j)),
            pl.BlockSpec((1, 1, 1, d), lambda i, j: (i, j, 0, 0)),
        ],
        out_specs=pl.BlockSpec((1, 1, rows, d), lambda i, j: (i, j, 0, 0)),
        out_shape=jax.ShapeDtypeStruct((depth, N_MOD, rows, d), F32),
        compiler_params=_cparams(2),
        name="adaln_modulation",
    )(c_rows, w_mod, b_mod.reshape(depth, N_MOD, 1, d))
    return out


def _ffn_kernel(x_ref, mod_ref, g_ref, wa_ref, wb_ref, w2_ref, o_ref, h_ref, acc_ref, *, k0):
    x = x_ref[...]
    mod = mod_ref[0]
    h_ref[...] = _modulate(x, g_ref[...], mod[k0:k0 + 1], mod[k0 + 1:k0 + 2]).astype(BF16)
    acc_ref[...] = jnp.zeros_like(acc_ref)

    def body(c, carry):
        h = h_ref[...]
        a = jnp.dot(h, wa_ref[c], preferred_element_type=F32)
        b = jnp.dot(h, wb_ref[c], preferred_element_type=F32)
        act = (a * jax.nn.sigmoid(a) * b).astype(BF16)
        acc_ref[...] += jnp.dot(act, w2_ref[c], preferred_element_type=F32)
        return carry

    lax.fori_loop(0, wa_ref.shape[0], body, 0)
    o_ref[...] = x + 0.5 * mod[k0 + 2:k0 + 3] * acc_ref[...]


def _half_ffn(xs, mods, g, wa, wb, w2, *, k0, n_tiles, group_of_tile):
    d = xs.shape[1]
    return pl.pallas_call(
        functools.partial(_ffn_kernel, k0=k0),
        grid=(n_tiles,),
        in_specs=[
            pl.BlockSpec((TM, d), lambda t: (t, 0)),
            pl.BlockSpec((1, N_MOD, d), lambda t: (group_of_tile(t), 0, 0)),
            pl.BlockSpec((1, d), lambda t: (0, 0)),
            _resident(wa.shape),
            _resident(wb.shape),
            _resident(w2.shape),
        ],
        out_specs=pl.BlockSpec((TM, d), lambda t: (t, 0)),
        out_shape=jax.ShapeDtypeStruct((n_tiles * TM, d), F32),
        scratch_shapes=[pltpu.VMEM((TM, d), BF16), pltpu.VMEM((TM, d), F32)],
        compiler_params=_cparams(1),
        name="half_ffn",
    )(xs, mods, g, wa, wb, w2)


def _rope(x, cos, sin_hi, sin_lo):
    return (x * cos + pltpu.roll(x, 16, axis=1) * sin_hi
            + pltpu.roll(x, LANES - 16, axis=1) * sin_lo)


def _proj_a_kernel(x_ref, mod_ref, g_ref, w_ref, gq_ref, gk_ref, hsum_ref,
                   cos_ref, shi_ref, slo_ref, qt_ref, k_ref, vt_ref, p_ref):
    mod = mod_ref[0]
    h = _modulate(x_ref[...], g_ref[...], mod[3:4], mod[4:5]).astype(BF16)
    z = jnp.dot(h, w_ref[...], preferred_element_type=F32)
    cos, shi, slo = cos_ref[...], shi_ref[...], slo_ref[...]
    hsum = hsum_ref[...]

    def norm_rope(zb, gain):
        ms = jnp.dot((zb * zb).astype(BF16), hsum, preferred_element_type=F32)
        return _rope(zb * lax.rsqrt(ms + EPS) * gain, cos, shi, slo)

    for j in range(ATTN_WIDTH // LANES):
        qb = norm_rope(z[:, j * LANES:(j + 1) * LANES], gq_ref[...])
        qt_ref[j * LANES:(j + 1) * LANES, :] = (qb * Q_SCALE_LOG2).T.astype(BF16)
    k_ref[...] = norm_rope(z[:, ATTN_WIDTH:ATTN_WIDTH + KV_WIDTH], gk_ref[...]).astype(BF16)
    vt_ref[0] = z[:, ATTN_WIDTH + KV_WIDTH:ATTN_WIDTH + 2 * KV_WIDTH].T.astype(BF16)
    p_ref[...] = z[:, ATTN_WIDTH + 2 * KV_WIDTH:]


def _project_a(xs, mods, g, w_in, gq, gk, hsum, cos, shi, slo, *, n_tiles, group_of_tile):
    d = xs.shape[1]
    nt = n_tiles * TM
    pool_width = w_in.shape[1] - ATTN_WIDTH - 2 * KV_WIDTH
    row_tile = lambda w: pl.BlockSpec((TM, w), lambda t: (t, 0))
    const2 = lambda shape: pl.BlockSpec(shape, lambda t: (0, 0))
    return pl.pallas_call(
        _proj_a_kernel,
        grid=(n_tiles,),
        in_specs=[
            row_tile(d),
            pl.BlockSpec((1, N_MOD, d), lambda t: (group_of_tile(t), 0, 0)),
            const2((1, d)),
            _resident(w_in.shape),
            const2((1, LANES)), const2((1, LANES)), const2((LANES, LANES)),
            row_tile(LANES), row_tile(LANES), row_tile(LANES),
        ],
        out_specs=[
            pl.BlockSpec((ATTN_WIDTH, TM), lambda t: (0, t)),
            row_tile(KV_WIDTH),
            pl.BlockSpec((1, KV_WIDTH, TM), lambda t: (t, 0, 0)),
            row_tile(pool_width),
        ],
        out_shape=[
            jax.ShapeDtypeStruct((ATTN_WIDTH, nt), BF16),
            jax.ShapeDtypeStruct((nt, KV_WIDTH), BF16),
            jax.ShapeDtypeStruct((n_tiles, KV_WIDTH, TM), BF16),
            jax.ShapeDtypeStruct((nt, pool_width), F32),
        ],
        compiler_params=_cparams(1),
        name="mixer_a_project",
    )(xs, mods, g, w_in, gq, gk, hsum, cos, shi, slo)


def _attn_kernel(*refs, n_lat_steps):
    if n_lat_steps:
        qt_ref, kc_ref, vtc_ref, kl_ref, vtl_ref, o_ref, m_ref, l_ref, acc_ref, s_ref = refs
    else:
        qt_ref, kc_ref, vtc_ref, o_ref, m_ref, l_ref, acc_ref = refs
    tq = qt_ref.shape[1]
    n_blocks = ATTN_WIDTH // LANES
    top = lax.broadcasted_iota(jnp.int32, (LANES, tq), 0) < HEAD_DIM
    pieces = []
    for j in range(n_blocks):
        blk = qt_ref[j * LANES:(j + 1) * LANES, :]
        zero = jnp.zeros_like(blk)
        pieces += [jnp.where(top, blk, zero), jnp.where(top, zero, blk)]
    qqt = jnp.concatenate(pieces, axis=1)

    m_ref[...] = jnp.full_like(m_ref, -jnp.inf)
    l_ref[...] = jnp.zeros_like(l_ref)
    acc_ref[...] = jnp.zeros_like(acc_ref)

    def scores(k):
        return jnp.dot(k, qqt, preferred_element_type=F32)

    def accumulate(s, vt):
        m_old = m_ref[...]
        m_new = jnp.maximum(m_old, jnp.max(s, axis=0, keepdims=True))
        alpha = jnp.exp2(m_old - m_new)
        p = jnp.exp2(s - m_new)
        l_ref[...] = alpha * l_ref[...] + jnp.sum(p, axis=0, keepdims=True)
        acc_ref[...] = alpha * acc_ref[...] + jnp.dot(vt, p.astype(BF16), preferred_element_type=F32)
        m_ref[...] = m_new

    def lat_keys(i):
        return kl_ref[pl.ds(pl.multiple_of(i * TK, TK), TK), :]

    if n_lat_steps:
        s_ref[1] = scores(lat_keys(0))
        accumulate(scores(kc_ref[...]), vtc_ref[0])

        def body(j, carry):
            i = 2 * j
            s_ref[0] = scores(lat_keys(i + 1))
            accumulate(s_ref[1], vtl_ref[i])
            s_ref[1] = scores(lat_keys(i + 2))
            accumulate(s_ref[0], vtl_ref[i + 1])
            return carry
        lax.fori_loop(0, n_lat_steps // 2 - 1, body, 0)
        s_ref[0] = scores(lat_keys(n_lat_steps - 1))
        accumulate(s_ref[1], vtl_ref[n_lat_steps - 2])
        accumulate(s_ref[0], vtl_ref[n_lat_steps - 1])
    else:
        accumulate(scores(kc_ref[...]), vtc_ref[0])

    out_t = acc_ref[...] / l_ref[...]
    for j in range(n_blocks):
        o_top = out_t[:, (2 * j) * tq:(2 * j + 1) * tq]
        o_bottom = out_t[:, (2 * j + 1) * tq:(2 * j + 2) * tq]
        o_ref[:, j * LANES:(j + 1) * LANES] = jnp.where(top, o_top, o_bottom).T.astype(o_ref.dtype)


def _attention(qt, k, vt, *, batch, seq, ctx_len, latent):
    lat_rows = batch * seq
    n_q = (seq if latent else ctx_len) // TQ
    q_blk0 = 0 if latent else lat_rows // TQ
    ctx_per_slab = TM // ctx_len
    ctx_slab0 = lat_rows // TM
    in_specs = [
        pl.BlockSpec((ATTN_WIDTH, TQ), lambda b, i: (0, q_blk0 + b * n_q + i)),
        pl.BlockSpec((ctx_len, KV_WIDTH), lambda b, i: (lat_rows // ctx_len + b, 0)),
        pl.BlockSpec((1, KV_WIDTH, ctx_len),
                     lambda b, i: (ctx_slab0 + b // ctx_per_slab, 0, b % ctx_per_slab)),
    ]
    operands = [qt, k, vt]
    if latent:
        in_specs += [pl.BlockSpec((seq, KV_WIDTH), lambda b, i: (b, 0)),
                     pl.BlockSpec((seq // TM, KV_WIDTH, TM), lambda b, i: (b, 0, 0))]
        operands += [k, vt]
    cols = N_Q_HEADS * TQ
    scratch = [pltpu.VMEM((1, cols), F32), pltpu.VMEM((1, cols), F32), pltpu.VMEM((KV_WIDTH, cols), F32)]
    if latent:
        assert (seq // TK) % 2 == 0 and seq // TK >= 2
        scratch.append(pltpu.VMEM((2, TK, cols), F32))
    return pl.pallas_call(
        functools.partial(_attn_kernel, n_lat_steps=seq // TK if latent else 0),
        grid=(batch, n_q),
        in_specs=in_specs,
        out_specs=pl.BlockSpec((TQ, ATTN_WIDTH), lambda b, i: (b * n_q + i, 0)),
        out_shape=jax.ShapeDtypeStruct((batch * n_q * TQ, ATTN_WIDTH), BF16),
        scratch_shapes=scratch,
        compiler_params=_cparams(2),
        name="gqa_attention_latent" if latent else "gqa_attention_ctx",
    )(*operands)


def _shift_rows(a, d):
    return pltpu.roll(a, d % a.shape[0], axis=0)


def _pool_group(pe, pos, n, w):
    zero = jnp.zeros_like(pe)
    left = jnp.where(pos >= 1, _shift_rows(pe, 1), zero)
    right = pe
    span = 1
    while 2 * span <= w // 2:
        left = left + jnp.where(pos >= span, _shift_rows(left, span), zero)
        right = right + jnp.where(pos + span < n, _shift_rows(right, -span), zero)
        span *= 2
    count = jnp.maximum(jnp.minimum(pos + w // 2, n) - jnp.maximum(pos - w // 2, 0), 1)
    return (left + right) / count.astype(F32)


def _rem_static(r, m):
    return r & (m - 1) if m & (m - 1) == 0 else lax.rem(r, m)


def _comb_a_kernel(x_ref, mod_ref, attn_ref, p_ref, pprev_ref, pnext_ref, pw_ref, ps_ref,
                   wo_ref, o_ref, cat_ref, *, lat_rows, seq, ctx_len):
    rows = TM + 2 * HALO
    row0 = pl.program_id(0) * TM - HALO
    r = row0 + lax.broadcasted_iota(jnp.int32, (rows, POOL_GROUP_DIM), 0)
    is_lat = r < lat_rows
    n = jnp.where(is_lat, seq, ctx_len)
    pos = jnp.where(is_lat, _rem_static(r, seq), _rem_static(r - lat_rows, ctx_len))

    cat_ref[:, :ATTN_WIDTH] = attn_ref[...]
    for gi, w in enumerate(POOL_WINDOWS):
        cols = slice(gi * POOL_GROUP_DIM, (gi + 1) * POOL_GROUP_DIM)
        pe = jnp.concatenate([pprev_ref[:, cols], p_ref[:, cols], pnext_ref[:, cols]], axis=0)
        centred = (_pool_group(pe, pos, n, w) - pe)[HALO:HALO + TM]
        pooled = jnp.dot(centred.astype(BF16), pw_ref[gi], preferred_element_type=F32) * ps_ref[:, cols]
        cat_ref[:, ATTN_WIDTH + gi * POOL_GROUP_DIM:ATTN_WIDTH + (gi + 1) * POOL_GROUP_DIM] = pooled.astype(BF16)
    mixed = jnp.dot(cat_ref[...], wo_ref[...], preferred_element_type=F32)
    o_ref[...] = x_ref[...] + mod_ref[0][5:6] * mixed


def _combine_a(xs, mods, attn, p, pool_w, pool_scale, w_out, *, n_tiles, group_of_tile,
               lat_rows, seq, ctx_len):
    d = xs.shape[1]
    pool_width = p.shape[1]
    halo_per_tile = TM // HALO
    last_halo_blk = p.shape[0] // HALO - 1
    return pl.pallas_call(
        functools.partial(_comb_a_kernel, lat_rows=lat_rows, seq=seq, ctx_len=ctx_len),
        grid=(n_tiles,),
        in_specs=[
            pl.BlockSpec((TM, d), lambda t: (t, 0)),
            pl.BlockSpec((1, N_MOD, d), lambda t: (group_of_tile(t), 0, 0)),
            pl.BlockSpec((TM, ATTN_WIDTH), lambda t: (t, 0)),
            pl.BlockSpec((TM, pool_width), lambda t: (t, 0)),
            pl.BlockSpec((HALO, pool_width), lambda t: (jnp.maximum(t * halo_per_tile - 1, 0), 0)),
            pl.BlockSpec((HALO, pool_width),
                         lambda t: (jnp.minimum((t + 1) * halo_per_tile, last_halo_blk), 0)),
            _resident(pool_w.shape),
            pl.BlockSpec((1, pool_width), lambda t: (0, 0)),
            _resident(w_out.shape),
        ],
        out_specs=pl.BlockSpec((TM, d), lambda t: (t, 0)),
        out_shape=jax.ShapeDtypeStruct((n_tiles * TM, d), F32),
        scratch_shapes=[pltpu.VMEM((TM, w_out.shape[0]), BF16)],
        compiler_params=_cparams(1),
        name="mixer_a_combine",
    )(xs, mods, attn, p, p, p, pool_w, pool_scale, w_out)


def _gelu_tanh(x):
    return 0.5 * x * (1.0 + jnp.tanh(np.sqrt(2.0 / np.pi).astype(np.float32) * (x + 0.044715 * (x * x * x))))


def _mixer_c_kernel(x_ref, mod_ref, g_ref, wi_ref, vg_ref, wsp_ref, bsp_ref, wo_ref, o_ref, gated_ref):
    x = x_ref[...]
    mod = mod_ref[0]
    width = wo_ref.shape[0]
    h = _modulate(x, g_ref[...], mod[3:4], mod[4:5]).astype(BF16)
    z = _gelu_tanh(jnp.dot(h, wi_ref[...], preferred_element_type=F32))
    u = z[:, :width]
    v = z[:, width:]
    v = (v * lax.rsqrt(jnp.mean(v * v, axis=-1, keepdims=True) + EPS) * vg_ref[...]).astype(BF16)
    bias = bsp_ref[...]
    for g in range(GMLP_GROUPS):
        cols = slice(g * CHUNK, (g + 1) * CHUNK)
        b_col = bias[:, g:g + 1]
        for c in range(TM // CHUNK):
            rows = slice(c * CHUNK, (c + 1) * CHUNK)
            sv = jnp.dot(wsp_ref[g], v[rows, cols], preferred_element_type=F32) + b_col
            gated_ref[rows, cols] = (u[rows, cols] * sv).astype(BF16)
    o_ref[...] = x + mod[5:6] * jnp.dot(gated_ref[...], wo_ref[...], preferred_element_type=F32)


def _mixer_c(xs, mods, g, w_in, v_g, w_sp, b_sp_cols, w_out, *, n_tiles, group_of_tile):
    d = xs.shape[1]
    width = w_out.shape[0]
    return pl.pallas_call(
        _mixer_c_kernel,
        grid=(n_tiles,),
        in_specs=[
            pl.BlockSpec((TM, d), lambda t: (t, 0)),
            pl.BlockSpec((1, N_MOD, d), lambda t: (group_of_tile(t), 0, 0)),
            pl.BlockSpec((1, d), lambda t: (0, 0)),
            _resident(w_in.shape),
            pl.BlockSpec((1, width), lambda t: (0, 0)),
            _resident(w_sp.shape),
            pl.BlockSpec(b_sp_cols.shape, lambda t: (0, 0)),
            _resident(w_out.shape),
        ],
        out_specs=pl.BlockSpec((TM, d), lambda t: (t, 0)),
        out_shape=jax.ShapeDtypeStruct((n_tiles * TM, d), F32),
        scratch_shapes=[pltpu.VMEM((TM, width), BF16)],
        compiler_params=_cparams(1),
        name="mixer_c_gmlp",
    )(xs, mods, g, w_in, v_g, w_sp, b_sp_cols, w_out)


def _rope_tables(batch, seq, n_ctx_rows):
    half = HEAD_DIM // 2
    pos = jnp.arange(seq)
    rows = (pos // GRID_W).astype(F32)
    cols = (pos % GRID_W).astype(F32)
    inv_freq = ROPE_THETA ** (-jnp.arange(0, half, 2, dtype=F32) / half)
    ang_r = rows[:, None] * inv_freq[None, :]
    ang_c = cols[:, None] * inv_freq[None, :]
    zeros = jnp.zeros_like(ang_r)
    cos = jnp.concatenate([jnp.cos(ang_r)] * 2 + [jnp.cos(ang_c)] * 2, axis=1)
    s_r, s_c = jnp.sin(ang_r), jnp.sin(ang_c)
    sin_hi = jnp.concatenate([zeros, s_r, zeros, s_c], axis=1)
    sin_lo = jnp.concatenate([-s_r, zeros, -s_c, zeros], axis=1)

    def stream(tab, ctx_value):
        lat = jnp.tile(tab, (batch, LANES // HEAD_DIM))
        return jnp.concatenate([lat, jnp.full((n_ctx_rows, LANES), ctx_value, F32)], axis=0)

    return stream(cos, 1.0), stream(sin_hi, 0.0), stream(sin_lo, 0.0)


def _head_block_order():
    order = []
    for j in range(Q_GROUP):
        order += [j, j + Q_GROUP]
    return np.concatenate([np.arange(h * HEAD_DIM, (h + 1) * HEAD_DIM) for h in order])


def kernel(x, c, ctx, c_ctx, w_mod, b_mod, norm_g, ffn_w13, ffn_w2, w_in_a, qk_norm_g, pool_w,
           pool_scale, w_out_a, w_in_c, v_norm_g, w_sp, b_sp, w_out_c):
    batch, seq, d = x.shape
    ctx_len = ctx.shape[1]
    depth = w_mod.shape[0]
    d_ff = ffn_w2.shape[2]
    lat_rows = batch * seq
    ctx_rows = batch * ctx_len
    assert seq % TM == 0 and ctx_rows % TM == 0 and seq % TK == 0 and ctx_len % TQ == 0
    assert ctx_len % HALO == 0 and lat_rows % ctx_len == 0 and d_ff % FF_CHUNK == 0
    assert TK == TM and TM % ctx_len == 0
    lat_tiles = lat_rows // TM
    all_tiles = lat_tiles + ctx_rows // TM
    tiles_per_batch = seq // TM
    group_of_tile = lambda t: jnp.minimum(t // tiles_per_batch, batch)

    c_rows = jnp.concatenate([c, c_ctx[None, :], jnp.zeros((SUBLANES - batch - 1, d), F32)], axis=0)
    mods_all = _modulation(c_rows, w_mod, b_mod)
    mods_all = jnp.transpose(mods_all[:, :, :batch + 1], (0, 2, 1, 3))

    n_ff = d_ff // FF_CHUNK

    def ffn_weights(i, j):
        w13 = ffn_w13[i, j].astype(BF16)
        wa = jnp.transpose(w13[:, :d_ff].reshape(d, n_ff, FF_CHUNK), (1, 0, 2))
        wb = jnp.transpose(w13[:, d_ff:].reshape(d, n_ff, FF_CHUNK), (1, 0, 2))
        return wa, wb, ffn_w2[i, j].astype(BF16).reshape(n_ff, FF_CHUNK, d)

    head_cols = _head_block_order()
    cos, sin_hi, sin_lo = _rope_tables(batch, seq, ctx_rows)
    head_mean = jnp.asarray(np.kron(np.eye(LANES // HEAD_DIM), np.full((HEAD_DIM, HEAD_DIM), 1.0 / HEAD_DIM)), BF16)

    xs = jnp.concatenate([x.reshape(lat_rows, d), ctx.reshape(ctx_rows, d)], axis=0)
    for i in range(depth):
        mods = mods_all[i]
        reads_ctx = i % 2 == 0
        later_reads_ctx = any(j % 2 == 0 for j in range(i + 1, depth))
        n1 = all_tiles if (reads_ctx or later_reads_ctx) else lat_tiles
        n2 = all_tiles if later_reads_ctx else lat_tiles
        common = dict(group_of_tile=group_of_tile)

        xs = _half_ffn(xs, mods, norm_g[i, 0][None], *ffn_weights(i, 0), k0=0, n_tiles=n1, **common)

        if reads_ctx:
            e = i // 2
            w_in = w_in_a[e]
            w_in = jnp.concatenate([w_in[:, :ATTN_WIDTH][:, head_cols], w_in[:, ATTN_WIDTH:]], axis=1).astype(BF16)
            w_out = jnp.concatenate([w_out_a[e][:ATTN_WIDTH][head_cols], w_out_a[e][ATTN_WIDTH:]], axis=0).astype(BF16)
            gq = jnp.tile(qk_norm_g[e, 0], LANES // HEAD_DIM)[None]
            gk = jnp.tile(qk_norm_g[e, 1], LANES // HEAD_DIM)[None]
            q, k, v, p = _project_a(xs, mods, norm_g[i, 1][None], w_in, gq, gk, head_mean,
                                    cos, sin_hi, sin_lo, n_tiles=n1, **common)
            attn = _attention(q, k, v, batch=batch, seq=seq, ctx_len=ctx_len, latent=True)
            if later_reads_ctx:
                attn_c = _attention(q, k, v, batch=batch, seq=seq, ctx_len=ctx_len, latent=False)
                attn = jnp.concatenate([attn, attn_c], axis=0)
            xs = _combine_a(xs, mods, attn, p, pool_w[e].astype(BF16), pool_scale[e][None], w_out,
                            n_tiles=n2, lat_rows=lat_rows, seq=seq, ctx_len=ctx_len, **common)
        else:
            o = i // 2
            xs = _mixer_c(xs, mods, norm_g[i, 1][None], w_in_c[o].astype(BF16), v_norm_g[o][None],
                          w_sp[o].astype(BF16), b_sp[o].T, w_out_c[o].astype(BF16), n_tiles=n2, **common)

        xs = _half_ffn(xs, mods, norm_g[i, 2][None], *ffn_weights(i, 1), k0=6, n_tiles=n2, **common)
    return xs[:lat_rows].reshape(batch, seq, d)
```

```python
import functools

import jax
import jax.numpy as jnp
import numpy as np
from jax import lax
from jax.experimental import pallas as pl
from jax.experimental.pallas import tpu as pltpu

F32 = jnp.float32
BF16 = jnp.bfloat16

GRID_W = 64
HEAD_DIM = 64
N_Q_HEADS = 8
N_KV_HEADS = 2
Q_GROUP = N_Q_HEADS // N_KV_HEADS
ATTN_WIDTH = N_Q_HEADS * HEAD_DIM
KV_WIDTH = N_KV_HEADS * HEAD_DIM
ATTN_SCALE = HEAD_DIM ** -0.5
Q_SCALE_LOG2 = float(ATTN_SCALE * np.log2(np.e))
ROPE_THETA = 10000.0
POOL_WINDOWS = (2, 4, 8, 16)
POOL_GROUP_DIM = 128
CHUNK = 128
GMLP_GROUPS = 8
N_MOD = 9
EPS = 1e-6

LANES = 128
SUBLANES = 8
VMEM_LIMIT_BYTES = 56 * 1024 * 1024

TM = 512
FF_CHUNK = 256
TQ = 128
TK = 512
HALO = 8


def _cparams(n_axes):
    return pltpu.CompilerParams(
        dimension_semantics=("arbitrary",) * n_axes,
        vmem_limit_bytes=VMEM_LIMIT_BYTES)


def _resident(shape):
    zeros = (0,) * len(shape)
    return pl.BlockSpec(shape, lambda *_: zeros, pipeline_mode=pl.Buffered(1))


def _modulate(x, g, shift, scale):
    ms = jnp.mean(x * x, axis=-1, keepdims=True)
    return (x * lax.rsqrt(ms + EPS) * g) * (1.0 + scale) + shift


def _mod_kernel(c_ref, w_ref, b_ref, o_ref):
    c = c_ref[...]
    s = (c * jax.nn.sigmoid(c)).astype(BF16)
    o_ref[0, 0] = jnp.dot(s, w_ref[0].astype(BF16), preferred_element_type=F32) + b_ref[0, 0]


def _modulation(c_rows, w_mod, b_mod):
    depth, d, _ = w_mod.shape
    rows = c_rows.shape[0]
    out = pl.pallas_call(
        _mod_kernel,
        grid=(depth, N_MOD),
        in_specs=[
            pl.BlockSpec((rows, d), lambda i, j: (0, 0)),
            pl.BlockSpec((1, d, d), lambda i, j: (i, 0, j)),
            pl.BlockSpec((1, 1, 1, d), lambda i, j: (i, j, 0, 0)),
        ],
        out_specs=pl.BlockSpec((1, 1, rows, d), lambda i, j: (i, j, 0, 0)),
        out_shape=jax.ShapeDtypeStruct((depth, N_MOD, rows, d), F32),
        compiler_params=_cparams(2),
        name="adaln_modulation",
    )(c_rows, w_mod, b_mod.reshape(depth, N_MOD, 1, d))
    return out


def _ffn_kernel(*refs, k0, head_tiles):
    if head_tiles is None:
        x_ref, mod_ref, g_ref, w13_ref, w2_ref, o_ref, h_ref, act_ref, acc_ref = refs
        x = x_ref[...]
    else:
        x_ref, tail_ref, mod_ref, g_ref, w13_ref, w2_ref, o_ref, h_ref, act_ref, acc_ref = refs
        x = jnp.where(pl.program_id(0) < head_tiles, x_ref[...], tail_ref[...])
    mod = mod_ref[0]
    d_ff = w2_ref.shape[2]
    n_chunks = d_ff // FF_CHUNK
    h_ref[...] = _modulate(x, g_ref[...], mod[k0:k0 + 1], mod[k0 + 1:k0 + 2]).astype(BF16)

    def chunk(c, offset=0):
        return pl.ds(pl.multiple_of(c * FF_CHUNK + offset, LANES), FF_CHUNK)

    def gate(slot, c):
        h = h_ref[...]
        a = jnp.dot(h, w13_ref[0, 0, :, chunk(c)].astype(BF16), preferred_element_type=F32)
        b = jnp.dot(h, w13_ref[0, 0, :, chunk(c, d_ff)].astype(BF16), preferred_element_type=F32)
        act_ref[slot] = (a * jax.nn.sigmoid(a) * b).astype(BF16)

    def down(slot, c):
        acc_ref[...] += jnp.dot(act_ref[slot], w2_ref[0, 0, chunk(c), :].astype(BF16),
                                preferred_element_type=F32)

    gate(0, 0)
    acc_ref[...] = jnp.zeros_like(acc_ref)

    def body(j, carry):
        c = 2 * j
        gate(1, c + 1)
        down(0, c)
        gate(0, c + 2)
        down(1, c + 1)
        return carry

    lax.fori_loop(0, (n_chunks - 1) // 2, body, 0)
    if n_chunks % 2 == 0:
        gate(1, n_chunks - 1)
        down(0, n_chunks - 2)
        down(1, n_chunks - 1)
    else:
        down(0, n_chunks - 1)
    o_ref[...] = x + 0.5 * mod[k0 + 2:k0 + 3] * acc_ref[...]


def _half_ffn(xs, mods, g, w13, w2, *, layer, half, n_tiles, group_of_tile, tail=None):
    d = xs.shape[1]
    k0 = 6 * half
    weight = lambda w: pl.BlockSpec((1, 1) + w.shape[2:], lambda t: (layer, half, 0, 0),
                                    pipeline_mode=pl.Buffered(1))
    if tail is None:
        head_tiles = None
        stream, stream_specs = [xs], [pl.BlockSpec((TM, d), lambda t: (t, 0))]
    else:
        head_tiles = xs.shape[0] // TM
        stream = [xs, tail]
        stream_specs = [pl.BlockSpec((TM, d), lambda t: (jnp.minimum(t, head_tiles - 1), 0)),
                        pl.BlockSpec((TM, d), lambda t: (jnp.maximum(t - head_tiles, 0), 0))]
    return pl.pallas_call(
        functools.partial(_ffn_kernel, k0=k0, head_tiles=head_tiles),
        grid=(n_tiles,),
        in_specs=stream_specs + [
            pl.BlockSpec((1, N_MOD, d), lambda t: (group_of_tile(t), 0, 0)),
            pl.BlockSpec((1, d), lambda t: (0, 0)),
            weight(w13),
            weight(w2),
        ],
        out_specs=pl.BlockSpec((TM, d), lambda t: (t, 0)),
        out_shape=jax.ShapeDtypeStruct((n_tiles * TM, d), F32),
        scratch_shapes=[pltpu.VMEM((TM, d), BF16), pltpu.VMEM((2, TM, FF_CHUNK), BF16),
                        pltpu.VMEM((TM, d), F32)],
        compiler_params=_cparams(1),
        name="half_ffn",
    )(*stream, mods, g, w13, w2)


def _rope(x, cos, sin_hi, sin_lo):
    return (x * cos + pltpu.roll(x, 16, axis=1) * sin_hi
            + pltpu.roll(x, LANES - 16, axis=1) * sin_lo)


def _proj_a_kernel(x_ref, mod_ref, g_ref, w_ref, gq_ref, gk_ref, hsum_ref,
                   rope_ref, qt_ref, k_ref, vt_ref, p_ref):
    mod = mod_ref[0]
    h = _modulate(x_ref[...], g_ref[...], mod[3:4], mod[4:5]).astype(BF16)
    z = jnp.dot(h, w_ref[...], preferred_element_type=F32)
    cos, shi, slo = (rope_ref[:, i * LANES:(i + 1) * LANES] for i in range(3))
    hsum = hsum_ref[...]

    def norm_rope(zb, gain):
        ms = jnp.dot((zb * zb).astype(BF16), hsum, preferred_element_type=F32)
        return _rope(zb * lax.rsqrt(ms + EPS) * gain, cos, shi, slo)

    for j in range(ATTN_WIDTH // LANES):
        qb = norm_rope(z[:, j * LANES:(j + 1) * LANES], gq_ref[...])
        qt_ref[j * LANES:(j + 1) * LANES, :] = (qb * Q_SCALE_LOG2).T.astype(BF16)
    k_ref[...] = norm_rope(z[:, ATTN_WIDTH:ATTN_WIDTH + KV_WIDTH], gk_ref[...]).astype(BF16)
    vt_ref[0] = z[:, ATTN_WIDTH + KV_WIDTH:ATTN_WIDTH + 2 * KV_WIDTH].T.astype(BF16)
    p_ref[...] = z[:, ATTN_WIDTH + 2 * KV_WIDTH:]


def _project_a(xs, mods, g, w_in, gq, gk, hsum, rope, *, n_tiles, group_of_tile, lat_tiles, tiles_per_batch):
    d = xs.shape[1]
    nt = n_tiles * TM
    rope_tile = lambda t: (jnp.where(t < lat_tiles, t % tiles_per_batch, tiles_per_batch), 0)
    pool_width = w_in.shape[1] - ATTN_WIDTH - 2 * KV_WIDTH
    row_tile = lambda w: pl.BlockSpec((TM, w), lambda t: (t, 0))
    const2 = lambda shape: pl.BlockSpec(shape, lambda t: (0, 0))
    return pl.pallas_call(
        _proj_a_kernel,
        grid=(n_tiles,),
        in_specs=[
            row_tile(d),
            pl.BlockSpec((1, N_MOD, d), lambda t: (group_of_tile(t), 0, 0)),
            const2((1, d)),
            _resident(w_in.shape),
            const2((1, LANES)), const2((1, LANES)), const2((LANES, LANES)),
            pl.BlockSpec((TM, 3 * LANES), rope_tile),
        ],
        out_specs=[
            pl.BlockSpec((ATTN_WIDTH, TM), lambda t: (0, t)),
            row_tile(KV_WIDTH),
            pl.BlockSpec((1, KV_WIDTH, TM), lambda t: (t, 0, 0)),
            row_tile(pool_width),
        ],
        out_shape=[
            jax.ShapeDtypeStruct((ATTN_WIDTH, nt), BF16),
            jax.ShapeDtypeStruct((nt, KV_WIDTH), BF16),
            jax.ShapeDtypeStruct((n_tiles, KV_WIDTH, TM), BF16),
            jax.ShapeDtypeStruct((nt, pool_width), F32),
        ],
        compiler_params=_cparams(1),
        name="mixer_a_project",
    )(xs, mods, g, w_in, gq, gk, hsum, rope)


def _attn_kernel(*refs, n_lat_steps):
    if n_lat_steps:
        qt_ref, kc_ref, vtc_ref, kl_ref, vtl_ref, o_ref, m_ref, l_ref, acc_ref, s_ref = refs
    else:
        qt_ref, kc_ref, vtc_ref, o_ref, m_ref, l_ref, acc_ref = refs
    tq = qt_ref.shape[1]
    n_blocks = ATTN_WIDTH // LANES
    top = lax.broadcasted_iota(jnp.int32, (LANES, tq), 0) < HEAD_DIM
    pieces = []
    for j in range(n_blocks):
        blk = qt_ref[j * LANES:(j + 1) * LANES, :]
        zero = jnp.zeros_like(blk)
        pieces += [jnp.where(top, blk, zero), jnp.where(top, zero, blk)]
    qqt = jnp.concatenate(pieces, axis=1)

    m_ref[...] = jnp.full_like(m_ref, -jnp.inf)
    l_ref[...] = jnp.zeros_like(l_ref)
    acc_ref[...] = jnp.zeros_like(acc_ref)

    def scores(k):
        return jnp.dot(k, qqt, preferred_element_type=F32)

    def accumulate(s, vt):
        m_old = m_ref[...]
        m_new = jnp.maximum(m_old, jnp.max(s, axis=0, keepdims=True))
        alpha = jnp.exp2(m_old - m_new)
        p = jnp.exp2(s - m_new)
        l_ref[...] = alpha * l_ref[...] + jnp.sum(p, axis=0, keepdims=True)
        acc_ref[...] = alpha * acc_ref[...] + jnp.dot(vt, p.astype(BF16), preferred_element_type=F32)
        m_ref[...] = m_new

    def lat_keys(i):
        return kl_ref[pl.ds(pl.multiple_of(i * TK, TK), TK), :]

    if n_lat_steps:
        s_ref[1] = scores(lat_keys(0))
        accumulate(scores(kc_ref[...]), vtc_ref[0])

        def body(j, carry):
            i = 2 * j
            s_ref[0] = scores(lat_keys(i + 1))
            accumulate(s_ref[1], vtl_ref[i])
            s_ref[1] = scores(lat_keys(i + 2))
            accumulate(s_ref[0], vtl_ref[i + 1])
            return carry
        lax.fori_loop(0, n_lat_steps // 2 - 1, body, 0)
        s_ref[0] = scores(lat_keys(n_lat_steps - 1))
        accumulate(s_ref[1], vtl_ref[n_lat_steps - 2])
        accumulate(s_ref[0], vtl_ref[n_lat_steps - 1])
    else:
        accumulate(scores(kc_ref[...]), vtc_ref[0])

    out_t = acc_ref[...] / l_ref[...]
    for j in range(n_blocks):
        o_top = out_t[:, (2 * j) * tq:(2 * j + 1) * tq]
        o_bottom = out_t[:, (2 * j + 1) * tq:(2 * j + 2) * tq]
        o_ref[:, j * LANES:(j + 1) * LANES] = jnp.where(top, o_top, o_bottom).T.astype(o_ref.dtype)


def _attention(qt, k, vt, *, batch, seq, ctx_len, latent):
    lat_rows = batch * seq
    n_q = (seq if latent else ctx_len) // TQ
    q_blk0 = 0 if latent else lat_rows // TQ
    ctx_per_slab = TM // ctx_len
    ctx_slab0 = lat_rows // TM
    in_specs = [
        pl.BlockSpec((ATTN_WIDTH, TQ), lambda b, i: (0, q_blk0 + b * n_q + i)),
        pl.BlockSpec((ctx_len, KV_WIDTH), lambda b, i: (lat_rows // ctx_len + b, 0)),
        pl.BlockSpec((1, KV_WIDTH, ctx_len),
                     lambda b, i: (ctx_slab0 + b // ctx_per_slab, 0, b % ctx_per_slab)),
    ]
    operands = [qt, k, vt]
    if latent:
        in_specs += [pl.BlockSpec((seq, KV_WIDTH), lambda b, i: (b, 0)),
                     pl.BlockSpec((seq // TM, KV_WIDTH, TM), lambda b, i: (b, 0, 0))]
        operands += [k, vt]
    cols = N_Q_HEADS * TQ
    scratch = [pltpu.VMEM((1, cols), F32), pltpu.VMEM((1, cols), F32), pltpu.VMEM((KV_WIDTH, cols), F32)]
    if latent:
        assert (seq // TK) % 2 == 0 and seq // TK >= 2
        scratch.append(pltpu.VMEM((2, TK, cols), F32))
    return pl.pallas_call(
        functools.partial(_attn_kernel, n_lat_steps=seq // TK if latent else 0),
        grid=(batch, n_q),
        in_specs=in_specs,
        out_specs=pl.BlockSpec((TQ, ATTN_WIDTH), lambda b, i: (b * n_q + i, 0)),
        out_shape=jax.ShapeDtypeStruct((batch * n_q * TQ, ATTN_WIDTH), BF16),
        scratch_shapes=scratch,
        compiler_params=_cparams(2),
        name="gqa_attention_latent" if latent else "gqa_attention_ctx",
    )(*operands)


def _shift_rows(a, d):
    return pltpu.roll(a, d % a.shape[0], axis=0)


def _pool_group(pe, pos, n, w):
    zero = jnp.zeros_like(pe)
    left = jnp.where(pos >= 1, _shift_rows(pe, 1), zero)
    right = pe
    span = 1
    while 2 * span <= w // 2:
        left = left + jnp.where(pos >= span, _shift_rows(left, span), zero)
        right = right + jnp.where(pos + span < n, _shift_rows(right, -span), zero)
        span *= 2
    count = jnp.maximum(jnp.minimum(pos + w // 2, n) - jnp.maximum(pos - w // 2, 0), 1)
    return (left + right) / count.astype(F32)


def _rem_static(r, m):
    return r & (m - 1) if m & (m - 1) == 0 else lax.rem(r, m)


def _comb_a_kernel(x_ref, mod_ref, attn_ref, p_ref, pprev_ref, pnext_ref, pw_ref, ps_ref,
                   wo_ref, o_ref, cat_ref, *, lat_rows, seq, ctx_len):
    rows = TM + 2 * HALO
    row0 = pl.program_id(0) * TM - HALO
    r = row0 + lax.broadcasted_iota(jnp.int32, (rows, POOL_GROUP_DIM), 0)
    is_lat = r < lat_rows
    n = jnp.where(is_lat, seq, ctx_len)
    pos = jnp.where(is_lat, _rem_static(r, seq), _rem_static(r - lat_rows, ctx_len))

    cat_ref[:, :ATTN_WIDTH] = attn_ref[...]
    for gi, w in enumerate(POOL_WINDOWS):
        cols = slice(gi * POOL_GROUP_DIM, (gi + 1) * POOL_GROUP_DIM)
        pe = jnp.concatenate([pprev_ref[:, cols], p_ref[:, cols], pnext_ref[:, cols]], axis=0)
        centred = (_pool_group(pe, pos, n, w) - pe)[HALO:HALO + TM]
        pooled = jnp.dot(centred.astype(BF16), pw_ref[gi], preferred_element_type=F32) * ps_ref[:, cols]
        cat_ref[:, ATTN_WIDTH + gi * POOL_GROUP_DIM:ATTN_WIDTH + (gi + 1) * POOL_GROUP_DIM] = pooled.astype(BF16)
    mixed = jnp.dot(cat_ref[...], wo_ref[...], preferred_element_type=F32)
    o_ref[...] = x_ref[...] + mod_ref[0][5:6] * mixed


def _combine_a(xs, mods, attn, p, pool_w, pool_scale, w_out, *, n_tiles, group_of_tile,
               lat_rows, seq, ctx_len):
    d = xs.shape[1]
    pool_width = p.shape[1]
    halo_per_tile = TM // HALO
    last_halo_blk = p.shape[0] // HALO - 1
    return pl.pallas_call(
        functools.partial(_comb_a_kernel, lat_rows=lat_rows, seq=seq, ctx_len=ctx_len),
        grid=(n_tiles,),
        in_specs=[
            pl.BlockSpec((TM, d), lambda t: (t, 0)),
            pl.BlockSpec((1, N_MOD, d), lambda t: (group_of_tile(t), 0, 0)),
            pl.BlockSpec((TM, ATTN_WIDTH), lambda t: (t, 0)),
            pl.BlockSpec((TM, pool_width), lambda t: (t, 0)),
            pl.BlockSpec((HALO, pool_width), lambda t: (jnp.maximum(t * halo_per_tile - 1, 0), 0)),
            pl.BlockSpec((HALO, pool_width),
                         lambda t: (jnp.minimum((t + 1) * halo_per_tile, last_halo_blk), 0)),
            _resident(pool_w.shape),
            pl.BlockSpec((1, pool_width), lambda t: (0, 0)),
            _resident(w_out.shape),
        ],
        out_specs=pl.BlockSpec((TM, d), lambda t: (t, 0)),
        out_shape=jax.ShapeDtypeStruct((n_tiles * TM, d), F32),
        scratch_shapes=[pltpu.VMEM((TM, w_out.shape[0]), BF16)],
        compiler_params=_cparams(1),
        name="mixer_a_combine",
    )(xs, mods, attn, p, p, p, pool_w, pool_scale, w_out)


def _gelu_tanh(x):
    return 0.5 * x * (1.0 + jnp.tanh(np.sqrt(2.0 / np.pi).astype(np.float32) * (x + 0.044715 * (x * x * x))))


def _mixer_c_kernel(x_ref, mod_ref, g_ref, wi_ref, vg_ref, wsp_ref, bsp_ref, wo_ref, o_ref, gated_ref):
    x = x_ref[...]
    mod = mod_ref[0]
    width = wo_ref.shape[0]
    h = _modulate(x, g_ref[...], mod[3:4], mod[4:5]).astype(BF16)
    z = _gelu_tanh(jnp.dot(h, wi_ref[...], preferred_element_type=F32))
    u = z[:, :width]
    v = z[:, width:]
    v = (v * lax.rsqrt(jnp.mean(v * v, axis=-1, keepdims=True) + EPS) * vg_ref[...]).astype(BF16)
    bias = bsp_ref[...]
    for g in range(GMLP_GROUPS):
        cols = slice(g * CHUNK, (g + 1) * CHUNK)
        b_col = bias[:, g:g + 1]
        for c in range(TM // CHUNK):
            rows = slice(c * CHUNK, (c + 1) * CHUNK)
            sv = jnp.dot(wsp_ref[g], v[rows, cols], preferred_element_type=F32) + b_col
            gated_ref[rows, cols] = (u[rows, cols] * sv).astype(BF16)
    o_ref[...] = x + mod[5:6] * jnp.dot(gated_ref[...], wo_ref[...], preferred_element_type=F32)


def _mixer_c(xs, mods, g, w_in, v_g, w_sp, b_sp_cols, w_out, *, n_tiles, group_of_tile):
    d = xs.shape[1]
    width = w_out.shape[0]
    return pl.pallas_call(
        _mixer_c_kernel,
        grid=(n_tiles,),
        in_specs=[
            pl.BlockSpec((TM, d), lambda t: (t, 0)),
            pl.BlockSpec((1, N_MOD, d), lambda t: (group_of_tile(t), 0, 0)),
            pl.BlockSpec((1, d), lambda t: (0, 0)),
            _resident(w_in.shape),
            pl.BlockSpec((1, width), lambda t: (0, 0)),
            _resident(w_sp.shape),
            pl.BlockSpec(b_sp_cols.shape, lambda t: (0, 0)),
            _resident(w_out.shape),
        ],
        out_specs=pl.BlockSpec((TM, d), lambda t: (t, 0)),
        out_shape=jax.ShapeDtypeStruct((n_tiles * TM, d), F32),
        scratch_shapes=[pltpu.VMEM((TM, width), BF16)],
        compiler_params=_cparams(1),
        name="mixer_c_gmlp",
    )(xs, mods, g, w_in, v_g, w_sp, b_sp_cols, w_out)


def _rope_table(seq):
    half = HEAD_DIM // 2
    pos = jnp.arange(seq)
    rows = (pos // GRID_W).astype(F32)
    cols = (pos % GRID_W).astype(F32)
    inv_freq = ROPE_THETA ** (-jnp.arange(0, half, 2, dtype=F32) / half)
    ang_r = rows[:, None] * inv_freq[None, :]
    ang_c = cols[:, None] * inv_freq[None, :]
    zeros = jnp.zeros_like(ang_r)
    cos = jnp.concatenate([jnp.cos(ang_r)] * 2 + [jnp.cos(ang_c)] * 2, axis=1)
    s_r, s_c = jnp.sin(ang_r), jnp.sin(ang_c)
    sin_hi = jnp.concatenate([zeros, s_r, zeros, s_c], axis=1)
    sin_lo = jnp.concatenate([-s_r, zeros, -s_c, zeros], axis=1)

    reps = LANES // HEAD_DIM
    lat = jnp.concatenate([cos] * reps + [sin_hi] * reps + [sin_lo] * reps, axis=1)
    still = jnp.concatenate([jnp.ones((TM, LANES), F32), jnp.zeros((TM, 2 * LANES), F32)], axis=1)
    return jnp.concatenate([lat, still], axis=0)


def _head_block_order():
    order = []
    for j in range(Q_GROUP):
        order += [j, j + Q_GROUP]
    return np.concatenate([np.arange(h * HEAD_DIM, (h + 1) * HEAD_DIM) for h in order])


def kernel(x, c, ctx, c_ctx, w_mod, b_mod, norm_g, ffn_w13, ffn_w2, w_in_a, qk_norm_g, pool_w,
           pool_scale, w_out_a, w_in_c, v_norm_g, w_sp, b_sp, w_out_c):
    batch, seq, d = x.shape
    ctx_len = ctx.shape[1]
    depth = w_mod.shape[0]
    d_ff = ffn_w2.shape[2]
    lat_rows = batch * seq
    ctx_rows = batch * ctx_len
    assert seq % TM == 0 and ctx_rows % TM == 0 and seq % TK == 0 and ctx_len % TQ == 0
    assert ctx_len % HALO == 0 and lat_rows % ctx_len == 0 and d_ff % FF_CHUNK == 0
    assert TK == TM and TM % ctx_len == 0
    lat_tiles = lat_rows // TM
    all_tiles = lat_tiles + ctx_rows // TM
    tiles_per_batch = seq // TM
    group_of_tile = lambda t: jnp.minimum(t // tiles_per_batch, batch)

    c_rows = jnp.concatenate([c, c_ctx[None, :], jnp.zeros((SUBLANES - batch - 1, d), F32)], axis=0)
    mods_all = _modulation(c_rows, w_mod, b_mod)
    mods_all = jnp.transpose(mods_all[:, :, :batch + 1], (0, 2, 1, 3))

    head_cols = _head_block_order()
    rope = _rope_table(seq)
    head_mean = jnp.asarray(np.kron(np.eye(LANES // HEAD_DIM), np.full((HEAD_DIM, HEAD_DIM), 1.0 / HEAD_DIM)), BF16)

    xs = x.reshape(lat_rows, d)
    tail = ctx.reshape(ctx_rows, d)
    for i in range(depth):
        mods = mods_all[i]
        reads_ctx = i % 2 == 0
        later_reads_ctx = any(j % 2 == 0 for j in range(i + 1, depth))
        n1 = all_tiles if (reads_ctx or later_reads_ctx) else lat_tiles
        n2 = all_tiles if later_reads_ctx else lat_tiles
        common = dict(group_of_tile=group_of_tile)

        xs = _half_ffn(xs, mods, norm_g[i, 0][None], ffn_w13, ffn_w2, layer=i, half=0, n_tiles=n1,
                       tail=tail if i == 0 else None, **common)

        if reads_ctx:
            e = i // 2
            w_in = w_in_a[e]
            w_in = jnp.concatenate([w_in[:, :ATTN_WIDTH][:, head_cols], w_in[:, ATTN_WIDTH:]], axis=1).astype(BF16)
            w_out = jnp.concatenate([w_out_a[e][:ATTN_WIDTH][head_cols], w_out_a[e][ATTN_WIDTH:]], axis=0).astype(BF16)
            gq = jnp.tile(qk_norm_g[e, 0], LANES // HEAD_DIM)[None]
            gk = jnp.tile(qk_norm_g[e, 1], LANES // HEAD_DIM)[None]
            q, k, v, p = _project_a(xs, mods, norm_g[i, 1][None], w_in, gq, gk, head_mean,
                                    rope, n_tiles=n1, lat_tiles=lat_tiles, tiles_per_batch=tiles_per_batch,
                                    **common)
            attn = _attention(q, k, v, batch=batch, seq=seq, ctx_len=ctx_len, latent=True)
            if later_reads_ctx:
                attn_c = _attention(q, k, v, batch=batch, seq=seq, ctx_len=ctx_len, latent=False)
                attn = jnp.concatenate([attn, attn_c], axis=0)
            xs = _combine_a(xs, mods, attn, p, pool_w[e].astype(BF16), pool_scale[e][None], w_out,
                            n_tiles=n2, lat_rows=lat_rows, seq=seq, ctx_len=ctx_len, **common)
        else:
            o = i // 2
            xs = _mixer_c(xs, mods, norm_g[i, 1][None], w_in_c[o].astype(BF16), v_norm_g[o][None],
                          w_sp[o].astype(BF16), b_sp[o].T, w_out_c[o].astype(BF16), n_tiles=n2, **common)

        xs = _half_ffn(xs, mods, norm_g[i, 2][None], ffn_w13, ffn_w2, layer=i, half=1, n_tiles=n2, **common)
    return xs[:lat_rows].reshape(batch, seq, d)
```

```python
import functools

import jax
import jax.numpy as jnp
import numpy as np
from jax import lax
from jax.experimental import pallas as pl
from jax.experimental.pallas import tpu as pltpu

F32 = jnp.float32
BF16 = jnp.bfloat16

GRID_W = 64
HEAD_DIM = 64
N_Q_HEADS = 8
N_KV_HEADS = 2
Q_GROUP = N_Q_HEADS // N_KV_HEADS
ATTN_WIDTH = N_Q_HEADS * HEAD_DIM
KV_WIDTH = N_KV_HEADS * HEAD_DIM
ATTN_SCALE = HEAD_DIM ** -0.5
Q_SCALE_LOG2 = float(ATTN_SCALE * np.log2(np.e))
ROPE_THETA = 10000.0
POOL_WINDOWS = (2, 4, 8, 16)
POOL_GROUP_DIM = 128
CHUNK = 128
GMLP_GROUPS = 8
N_MOD = 9
EPS = 1e-6

LANES = 128
SUBLANES = 8
VMEM_LIMIT_BYTES = 56 * 1024 * 1024

TM = 512
FF_CHUNK = 256
TQ = 128
TK = 512
HALO = 8


def _cparams(n_axes):
    return pltpu.CompilerParams(
        dimension_semantics=("arbitrary",) * n_axes,
        vmem_limit_bytes=VMEM_LIMIT_BYTES)


def _resident(shape):
    zeros = (0,) * len(shape)
    return pl.BlockSpec(shape, lambda *_: zeros, pipeline_mode=pl.Buffered(1))


def _modulate(x, g, shift, scale):
    ms = jnp.mean(x * x, axis=-1, keepdims=True)
    return (x * lax.rsqrt(ms + EPS) * g) * (1.0 + scale) + shift


def _mod_kernel(c_ref, w_ref, b_ref, o_ref):
    c = c_ref[...]
    s = (c * jax.nn.sigmoid(c)).astype(BF16)
    o_ref[0, 0] = jnp.dot(s, w_ref[0].astype(BF16), preferred_element_type=F32) + b_ref[0, 0]


def _modulation(c_rows, w_mod, b_mod):
    depth, d, _ = w_mod.shape
    rows = c_rows.shape[0]
    out = pl.pallas_call(
        _mod_kernel,
        grid=(depth, N_MOD),
        in_specs=[
            pl.BlockSpec((rows, d), lambda i, j: (0, 0)),
            pl.BlockSpec((1, d, d), lambda i, j: (i, 0, j)),
            pl.BlockSpec((1, 1, 1, d), lambda i, j: (i, j, 0, 0)),
        ],
        out_specs=pl.BlockSpec((1, 1, rows, d), lambda i, j: (i, j, 0, 0)),
        out_shape=jax.ShapeDtypeStruct((depth, N_MOD, rows, d), F32),
        compiler_params=_cparams(2),
        name="adaln_modulation",
    )(c_rows, w_mod, b_mod.reshape(depth, N_MOD, 1, d))
    return out


def _ffn_kernel(*refs, k0, head_tiles):
    if head_tiles is None:
        x_ref, mod_ref, g_ref, w13_ref, w2_ref, o_ref, h_ref, act_ref, acc_ref = refs
        x = x_ref[...]
    else:
        x_ref, tail_ref, mod_ref, g_ref, w13_ref, w2_ref, o_ref, h_ref, act_ref, acc_ref = refs
        x = jnp.where(pl.program_id(0) < head_tiles, x_ref[...], tail_ref[...])
    mod = mod_ref[0]
    d_ff = w2_ref.shape[2]
    n_chunks = d_ff // FF_CHUNK
    h_ref[...] = _modulate(x, g_ref[...], mod[k0:k0 + 1], mod[k0 + 1:k0 + 2]).astype(BF16)

    def chunk(c, offset=0):
        return pl.ds(pl.multiple_of(c * FF_CHUNK + offset, LANES), FF_CHUNK)

    def gate(slot, c):
        h = h_ref[...]
        a = jnp.dot(h, w13_ref[0, 0, :, chunk(c)].astype(BF16), preferred_element_type=F32)
        b = jnp.dot(h, w13_ref[0, 0, :, chunk(c, d_ff)].astype(BF16), preferred_element_type=F32)
        act_ref[slot] = (a * jax.nn.sigmoid(a) * b).astype(BF16)

    def down(slot, c):
        acc_ref[...] += jnp.dot(act_ref[slot], w2_ref[0, 0, chunk(c), :].astype(BF16),
                                preferred_element_type=F32)

    gate(0, 0)
    acc_ref[...] = jnp.zeros_like(acc_ref)

    def body(j, carry):
        c = 2 * j
        gate(1, c + 1)
        down(0, c)
        gate(0, c + 2)
        down(1, c + 1)
        return carry

    lax.fori_loop(0, (n_chunks - 1) // 2, body, 0)
    if n_chunks % 2 == 0:
        gate(1, n_chunks - 1)
        down(0, n_chunks - 2)
        down(1, n_chunks - 1)
    else:
        down(0, n_chunks - 1)
    o_ref[...] = x + 0.5 * mod[k0 + 2:k0 + 3] * acc_ref[...]


def _half_ffn(xs, mods, g, w13, w2, *, layer, half, n_tiles, group_of_tile, tail=None):
    d = xs.shape[1]
    k0 = 6 * half
    weight = lambda w: pl.BlockSpec((1, 1) + w.shape[2:], lambda t: (layer, half, 0, 0),
                                    pipeline_mode=pl.Buffered(1))
    if tail is None:
        head_tiles = None
        stream, stream_specs = [xs], [pl.BlockSpec((TM, d), lambda t: (t, 0))]
    else:
        head_tiles = xs.shape[0] // TM
        stream = [xs, tail]
        stream_specs = [pl.BlockSpec((TM, d), lambda t: (jnp.minimum(t, head_tiles - 1), 0)),
                        pl.BlockSpec((TM, d), lambda t: (jnp.maximum(t - head_tiles, 0), 0))]
    return pl.pallas_call(
        functools.partial(_ffn_kernel, k0=k0, head_tiles=head_tiles),
        grid=(n_tiles,),
        in_specs=stream_specs + [
            pl.BlockSpec((1, N_MOD, d), lambda t: (group_of_tile(t), 0, 0)),
            pl.BlockSpec((1, d), lambda t: (0, 0)),
            weight(w13),
            weight(w2),
        ],
        out_specs=pl.BlockSpec((TM, d), lambda t: (t, 0)),
        out_shape=jax.ShapeDtypeStruct((n_tiles * TM, d), F32),
        scratch_shapes=[pltpu.VMEM((TM, d), BF16), pltpu.VMEM((2, TM, FF_CHUNK), BF16),
                        pltpu.VMEM((TM, d), F32)],
        compiler_params=_cparams(1),
        name="half_ffn",
    )(*stream, mods, g, w13, w2)


def _rope(x, cos, sin_hi, sin_lo):
    return (x * cos + pltpu.roll(x, 16, axis=1) * sin_hi
            + pltpu.roll(x, LANES - 16, axis=1) * sin_lo)


def _proj_a_kernel(x_ref, mod_ref, g_ref, w_ref, gq_ref, gk_ref, hsum_ref,
                   rope_row_ref, rope_col_ref, qt_ref, k_ref, vt_ref, p_ref, *, lat_tiles):
    mod = mod_ref[0]
    h = _modulate(x_ref[...], g_ref[...], mod[3:4], mod[4:5]).astype(BF16)
    z = jnp.dot(h, w_ref[...], preferred_element_type=F32)

    by_row, by_col = rope_row_ref[...], rope_col_ref[...]
    rope = jnp.concatenate([jnp.broadcast_to(by_row[r:r + 1, :], by_col.shape) + by_col
                            for r in range(TM // GRID_W)], axis=0)
    is_ctx = pl.program_id(0) >= lat_tiles
    cos = jnp.where(is_ctx, 1.0, rope[:, :LANES])
    shi = jnp.where(is_ctx, 0.0, rope[:, LANES:2 * LANES])
    slo = jnp.where(is_ctx, 0.0, rope[:, 2 * LANES:])
    hsum = hsum_ref[...]

    def norm_rope(zb, gain):
        ms = jnp.dot((zb * zb).astype(BF16), hsum, preferred_element_type=F32)
        return _rope(zb * lax.rsqrt(ms + EPS) * gain, cos, shi, slo)

    for j in range(ATTN_WIDTH // LANES):
        qb = norm_rope(z[:, j * LANES:(j + 1) * LANES], gq_ref[...])
        qt_ref[j * LANES:(j + 1) * LANES, :] = (qb * Q_SCALE_LOG2).T.astype(BF16)
    k_ref[...] = norm_rope(z[:, ATTN_WIDTH:ATTN_WIDTH + KV_WIDTH], gk_ref[...]).astype(BF16)
    vt_ref[0] = z[:, ATTN_WIDTH + KV_WIDTH:ATTN_WIDTH + 2 * KV_WIDTH].T.astype(BF16)
    p_ref[...] = z[:, ATTN_WIDTH + 2 * KV_WIDTH:]


def _project_a(xs, mods, g, w_in, gq, gk, hsum, rope_by_row, rope_by_col, *, n_tiles, group_of_tile,
               lat_tiles, tiles_per_batch):
    d = xs.shape[1]
    nt = n_tiles * TM
    grid_rows = TM // GRID_W
    pool_width = w_in.shape[1] - ATTN_WIDTH - 2 * KV_WIDTH
    row_tile = lambda w: pl.BlockSpec((TM, w), lambda t: (t, 0))
    const2 = lambda shape: pl.BlockSpec(shape, lambda t: (0, 0))
    return pl.pallas_call(
        functools.partial(_proj_a_kernel, lat_tiles=lat_tiles),
        grid=(n_tiles,),
        in_specs=[
            row_tile(d),
            pl.BlockSpec((1, N_MOD, d), lambda t: (group_of_tile(t), 0, 0)),
            const2((1, d)),
            _resident(w_in.shape),
            const2((1, LANES)), const2((1, LANES)), const2((LANES, LANES)),
            pl.BlockSpec((grid_rows, 3 * LANES), lambda t: (t % tiles_per_batch, 0)),
            const2(rope_by_col.shape),
        ],
        out_specs=[
            pl.BlockSpec((ATTN_WIDTH, TM), lambda t: (0, t)),
            row_tile(KV_WIDTH),
            pl.BlockSpec((1, KV_WIDTH, TM), lambda t: (t, 0, 0)),
            row_tile(pool_width),
        ],
        out_shape=[
            jax.ShapeDtypeStruct((ATTN_WIDTH, nt), BF16),
            jax.ShapeDtypeStruct((nt, KV_WIDTH), BF16),
            jax.ShapeDtypeStruct((n_tiles, KV_WIDTH, TM), BF16),
            jax.ShapeDtypeStruct((nt, pool_width), F32),
        ],
        compiler_params=_cparams(1),
        name="mixer_a_project",
    )(xs, mods, g, w_in, gq, gk, hsum, rope_by_row, rope_by_col)


def _attn_kernel(*refs, n_lat_steps):
    if n_lat_steps:
        qt_ref, kc_ref, vtc_ref, kl_ref, vtl_ref, o_ref, m_ref, l_ref, acc_ref, s_ref = refs
    else:
        qt_ref, kc_ref, vtc_ref, o_ref, m_ref, l_ref, acc_ref = refs
    tq = qt_ref.shape[1]
    n_blocks = ATTN_WIDTH // LANES
    top = lax.broadcasted_iota(jnp.int32, (LANES, tq), 0) < HEAD_DIM
    pieces = []
    for j in range(n_blocks):
        blk = qt_ref[j * LANES:(j + 1) * LANES, :]
        zero = jnp.zeros_like(blk)
        pieces += [jnp.where(top, blk, zero), jnp.where(top, zero, blk)]
    qqt = jnp.concatenate(pieces, axis=1)

    m_ref[...] = jnp.full_like(m_ref, -jnp.inf)
    l_ref[...] = jnp.zeros_like(l_ref)
    acc_ref[...] = jnp.zeros_like(acc_ref)

    def scores(k):
        return jnp.dot(k, qqt, preferred_element_type=F32)

    def accumulate(s, vt):
        m_old = m_ref[...]
        m_new = jnp.maximum(m_old, jnp.max(s, axis=0, keepdims=True))
        alpha = jnp.exp2(m_old - m_new)
        p = jnp.exp2(s - m_new)
        l_ref[...] = alpha * l_ref[...] + jnp.sum(p, axis=0, keepdims=True)
        acc_ref[...] = alpha * acc_ref[...] + jnp.dot(vt, p.astype(BF16), preferred_element_type=F32)
        m_ref[...] = m_new

    def lat_keys(i):
        return kl_ref[pl.ds(pl.multiple_of(i * TK, TK), TK), :]

    s_ctx = scores(kc_ref[...])
    if n_lat_steps:
        s_ref[1] = scores(lat_keys(0))
        accumulate(s_ctx, vtc_ref[0])

        def body(j, carry):
            i = 2 * j
            s_ref[0] = scores(lat_keys(i + 1))
            accumulate(s_ref[1], vtl_ref[i])
            s_ref[1] = scores(lat_keys(i + 2))
            accumulate(s_ref[0], vtl_ref[i + 1])
            return carry
        lax.fori_loop(0, n_lat_steps // 2 - 1, body, 0)
        s_ref[0] = scores(lat_keys(n_lat_steps - 1))
        accumulate(s_ref[1], vtl_ref[n_lat_steps - 2])
        accumulate(s_ref[0], vtl_ref[n_lat_steps - 1])
    else:
        accumulate(s_ctx, vtc_ref[0])

    out_t = acc_ref[...] / l_ref[...]
    for j in range(n_blocks):
        o_top = out_t[:, (2 * j) * tq:(2 * j + 1) * tq]
        o_bottom = out_t[:, (2 * j + 1) * tq:(2 * j + 2) * tq]
        o_ref[:, j * LANES:(j + 1) * LANES] = jnp.where(top, o_top, o_bottom).T.astype(o_ref.dtype)


def _attention(qt, k, vt, *, batch, seq, ctx_len, latent):
    lat_rows = batch * seq
    n_q = (seq if latent else ctx_len) // TQ
    q_blk0 = 0 if latent else lat_rows // TQ
    ctx_per_slab = TM // ctx_len
    ctx_slab0 = lat_rows // TM
    in_specs = [
        pl.BlockSpec((ATTN_WIDTH, TQ), lambda b, i: (0, q_blk0 + b * n_q + i)),
        pl.BlockSpec((ctx_len, KV_WIDTH), lambda b, i: (lat_rows // ctx_len + b, 0)),
        pl.BlockSpec((1, KV_WIDTH, ctx_len),
                     lambda b, i: (ctx_slab0 + b // ctx_per_slab, 0, b % ctx_per_slab)),
    ]
    operands = [qt, k, vt]
    if latent:
        in_specs += [pl.BlockSpec((seq, KV_WIDTH), lambda b, i: (b, 0)),
                     pl.BlockSpec((seq // TM, KV_WIDTH, TM), lambda b, i: (b, 0, 0))]
        operands += [k, vt]
    cols = N_Q_HEADS * TQ
    scratch = [pltpu.VMEM((1, cols), F32), pltpu.VMEM((1, cols), F32), pltpu.VMEM((KV_WIDTH, cols), F32)]
    if latent:
        assert (seq // TK) % 2 == 0 and seq // TK >= 2
        scratch.append(pltpu.VMEM((2, TK, cols), F32))
    return pl.pallas_call(
        functools.partial(_attn_kernel, n_lat_steps=seq // TK if latent else 0),
        grid=(batch, n_q),
        in_specs=in_specs,
        out_specs=pl.BlockSpec((TQ, ATTN_WIDTH), lambda b, i: (b * n_q + i, 0)),
        out_shape=jax.ShapeDtypeStruct((batch * n_q * TQ, ATTN_WIDTH), BF16),
        scratch_shapes=scratch,
        compiler_params=_cparams(2),
        name="gqa_attention_latent" if latent else "gqa_attention_ctx",
    )(*operands)


def _shift_rows(a, d):
    return pltpu.roll(a, d % a.shape[0], axis=0)


def _pool_group(pe, w, edge):
    def before(a, span):
        shifted = _shift_rows(a, span)
        return shifted if edge is None else jnp.where(edge[0] >= span, shifted, jnp.zeros_like(a))

    def after(a, span):
        shifted = _shift_rows(a, -span)
        return shifted if edge is None else jnp.where(edge[0] + span < edge[1], shifted, jnp.zeros_like(a))

    left = before(pe, 1)
    right = pe
    span = 1
    while 2 * span <= w // 2:
        left = left + before(left, span)
        right = right + after(right, span)
        span *= 2
    if edge is None:
        return (left + right) * (1.0 / w)
    pos, n = edge
    count = jnp.maximum(jnp.minimum(pos + w // 2, n) - jnp.maximum(pos - w // 2, 0), 1)
    return (left + right) / count.astype(F32)


def _rem_static(r, m):
    return r & (m - 1) if m & (m - 1) == 0 else lax.rem(r, m)


def _comb_a_kernel(x_ref, mod_ref, attn_ref, attn_tail_ref, p_ref, pprev_ref, pnext_ref, pw_ref, ps_ref,
                   wo_ref, o_ref, pooled_ref, *, lat_rows, seq, ctx_len, attn_head_tiles):
    tile = pl.program_id(0)
    tiles_per_seq = seq // TM

    def mix(edge):
        attn = jnp.where(tile < attn_head_tiles, attn_ref[...], attn_tail_ref[...])
        mixed = jnp.dot(attn, wo_ref[:ATTN_WIDTH, :], preferred_element_type=F32)
        for gi, w in enumerate(POOL_WINDOWS):
            cols = slice(gi * POOL_GROUP_DIM, (gi + 1) * POOL_GROUP_DIM)
            pe = jnp.concatenate([pprev_ref[:, cols], p_ref[:, cols], pnext_ref[:, cols]], axis=0)
            centred = (_pool_group(pe, w, edge) - pe)[HALO:HALO + TM]
            pooled = jnp.dot(centred.astype(BF16), pw_ref[gi], preferred_element_type=F32) * ps_ref[:, cols]
            pooled_ref[:, cols] = pooled.astype(BF16)
        mixed += jnp.dot(pooled_ref[...], wo_ref[ATTN_WIDTH:, :], preferred_element_type=F32)
        o_ref[...] = x_ref[...] + mod_ref[0][5:6] * mixed

    seq_tile = lax.rem(tile, tiles_per_seq)
    at_edge = (tile >= lat_rows // TM) | (seq_tile == 0) | (seq_tile == tiles_per_seq - 1)

    @pl.when(at_edge)
    def _():
        rows = TM + 2 * HALO
        r = tile * TM - HALO + lax.broadcasted_iota(jnp.int32, (rows, POOL_GROUP_DIM), 0)
        is_lat = r < lat_rows
        n = jnp.where(is_lat, seq, ctx_len)
        pos = jnp.where(is_lat, _rem_static(r, seq), _rem_static(r - lat_rows, ctx_len))
        mix((pos, n))

    @pl.when(jnp.logical_not(at_edge))
    def _():
        mix(None)


def _combine_a(xs, mods, attn, attn_tail, p, pool_w, pool_scale, w_out, *, n_tiles, group_of_tile,
               lat_rows, seq, ctx_len):
    d = xs.shape[1]
    pool_width = p.shape[1]
    halo_per_tile = TM // HALO
    last_halo_blk = p.shape[0] // HALO - 1
    head_tiles = attn.shape[0] // TM
    return pl.pallas_call(
        functools.partial(_comb_a_kernel, lat_rows=lat_rows, seq=seq, ctx_len=ctx_len,
                          attn_head_tiles=head_tiles),
        grid=(n_tiles,),
        in_specs=[
            pl.BlockSpec((TM, d), lambda t: (t, 0)),
            pl.BlockSpec((1, N_MOD, d), lambda t: (group_of_tile(t), 0, 0)),
            pl.BlockSpec((TM, ATTN_WIDTH), lambda t: (jnp.minimum(t, head_tiles - 1), 0)),
            pl.BlockSpec((TM, ATTN_WIDTH), lambda t: (jnp.maximum(t - head_tiles, 0), 0)),
            pl.BlockSpec((TM, pool_width), lambda t: (t, 0)),
            pl.BlockSpec((HALO, pool_width), lambda t: (jnp.maximum(t * halo_per_tile - 1, 0), 0)),
            pl.BlockSpec((HALO, pool_width),
                         lambda t: (jnp.minimum((t + 1) * halo_per_tile, last_halo_blk), 0)),
            _resident(pool_w.shape),
            pl.BlockSpec((1, pool_width), lambda t: (0, 0)),
            _resident(w_out.shape),
        ],
        out_specs=pl.BlockSpec((TM, d), lambda t: (t, 0)),
        out_shape=jax.ShapeDtypeStruct((n_tiles * TM, d), F32),
        scratch_shapes=[pltpu.VMEM((TM, pool_width), BF16)],
        compiler_params=_cparams(1),
        name="mixer_a_combine",
    )(xs, mods, attn, attn_tail, p, p, p, pool_w, pool_scale, w_out)


def _gelu_tanh(x):
    return 0.5 * x * (1.0 + jnp.tanh(np.sqrt(2.0 / np.pi).astype(np.float32) * (x + 0.044715 * (x * x * x))))


def _mixer_c_kernel(x_ref, mod_ref, g_ref, wi_ref, vg_ref, wsp_ref, bsp_ref, wo_ref, o_ref, gated_ref):
    x = x_ref[...]
    mod = mod_ref[0]
    width = wo_ref.shape[0]
    h = _modulate(x, g_ref[...], mod[3:4], mod[4:5]).astype(BF16)
    z = _gelu_tanh(jnp.dot(h, wi_ref[...], preferred_element_type=F32))
    u = z[:, :width]
    v = z[:, width:]
    v = (v * lax.rsqrt(jnp.mean(v * v, axis=-1, keepdims=True) + EPS) * vg_ref[...]).astype(BF16)
    bias = bsp_ref[...]
    for g in range(GMLP_GROUPS):
        cols = slice(g * CHUNK, (g + 1) * CHUNK)
        b_col = bias[:, g:g + 1]
        for c in range(TM // CHUNK):
            rows = slice(c * CHUNK, (c + 1) * CHUNK)
            sv = jnp.dot(wsp_ref[g], v[rows, cols], preferred_element_type=F32) + b_col
            gated_ref[rows, cols] = (u[rows, cols] * sv).astype(BF16)
    o_ref[...] = x + mod[5:6] * jnp.dot(gated_ref[...], wo_ref[...], preferred_element_type=F32)


def _mixer_c(xs, mods, g, w_in, v_g, w_sp, b_sp_cols, w_out, *, n_tiles, group_of_tile):
    d = xs.shape[1]
    width = w_out.shape[0]
    return pl.pallas_call(
        _mixer_c_kernel,
        grid=(n_tiles,),
        in_specs=[
            pl.BlockSpec((TM, d), lambda t: (t, 0)),
            pl.BlockSpec((1, N_MOD, d), lambda t: (group_of_tile(t), 0, 0)),
            pl.BlockSpec((1, d), lambda t: (0, 0)),
            _resident(w_in.shape),
            pl.BlockSpec((1, width), lambda t: (0, 0)),
            _resident(w_sp.shape),
            pl.BlockSpec(b_sp_cols.shape, lambda t: (0, 0)),
            _resident(w_out.shape),
        ],
        out_specs=pl.BlockSpec((TM, d), lambda t: (t, 0)),
        out_shape=jax.ShapeDtypeStruct((n_tiles * TM, d), F32),
        scratch_shapes=[pltpu.VMEM((TM, width), BF16)],
        compiler_params=_cparams(1),
        name="mixer_c_gmlp",
    )(xs, mods, g, w_in, v_g, w_sp, b_sp_cols, w_out)


def _rope_tables(seq):
    half = HEAD_DIM // 2
    inv_freq = ROPE_THETA ** (-jnp.arange(0, half, 2, dtype=F32) / half)

    def parts(n, row_axis):
        ang = jnp.arange(n).astype(F32)[:, None] * inv_freq[None, :]
        cos, sin, zero = jnp.cos(ang), jnp.sin(ang), jnp.zeros_like(ang)
        if row_axis:
            heads = ([cos, cos, zero, zero], [zero, sin, zero, zero], [-sin, zero, zero, zero])
        else:
            heads = ([zero, zero, cos, cos], [zero, zero, zero, sin], [zero, zero, -sin, zero])
        return jnp.concatenate([blk for head in heads for blk in head * (LANES // HEAD_DIM)], axis=1)

    return parts(seq // GRID_W, True), parts(GRID_W, False)


def _head_block_order():
    order = []
    for j in range(Q_GROUP):
        order += [j, j + Q_GROUP]
    return np.concatenate([np.arange(h * HEAD_DIM, (h + 1) * HEAD_DIM) for h in order])


def kernel(x, c, ctx, c_ctx, w_mod, b_mod, norm_g, ffn_w13, ffn_w2, w_in_a, qk_norm_g, pool_w,
           pool_scale, w_out_a, w_in_c, v_norm_g, w_sp, b_sp, w_out_c):
    batch, seq, d = x.shape
    ctx_len = ctx.shape[1]
    depth = w_mod.shape[0]
    d_ff = ffn_w2.shape[2]
    lat_rows = batch * seq
    ctx_rows = batch * ctx_len
    assert seq % TM == 0 and ctx_rows % TM == 0 and seq % TK == 0 and ctx_len % TQ == 0
    assert ctx_len % HALO == 0 and lat_rows % ctx_len == 0 and d_ff % FF_CHUNK == 0
    assert TK == TM and TM % ctx_len == 0
    assert seq % GRID_W == 0 and TM % (GRID_W * SUBLANES) == 0
    lat_tiles = lat_rows // TM
    all_tiles = lat_tiles + ctx_rows // TM
    tiles_per_batch = seq // TM
    group_of_tile = lambda t: jnp.minimum(t // tiles_per_batch, batch)

    c_rows = jnp.concatenate([c, c_ctx[None, :], jnp.zeros((SUBLANES - batch - 1, d), F32)], axis=0)
    mods_all = _modulation(c_rows, w_mod, b_mod)
    mods_all = jnp.transpose(mods_all[:, :, :batch + 1], (0, 2, 1, 3))

    head_cols = _head_block_order()
    rope_by_row, rope_by_col = _rope_tables(seq)
    head_mean = jnp.asarray(np.kron(np.eye(LANES // HEAD_DIM), np.full((HEAD_DIM, HEAD_DIM), 1.0 / HEAD_DIM)), BF16)

    xs = x.reshape(lat_rows, d)
    tail = ctx.reshape(ctx_rows, d)
    for i in range(depth):
        mods = mods_all[i]
        reads_ctx = i % 2 == 0
        later_reads_ctx = any(j % 2 == 0 for j in range(i + 1, depth))
        n1 = all_tiles if (reads_ctx or later_reads_ctx) else lat_tiles
        n2 = all_tiles if later_reads_ctx else lat_tiles
        common = dict(group_of_tile=group_of_tile)

        xs = _half_ffn(xs, mods, norm_g[i, 0][None], ffn_w13, ffn_w2, layer=i, half=0, n_tiles=n1,
                       tail=tail if i == 0 else None, **common)

        if reads_ctx:
            e = i // 2
            w_in = w_in_a[e]
            w_in = jnp.concatenate([w_in[:, :ATTN_WIDTH][:, head_cols], w_in[:, ATTN_WIDTH:]], axis=1).astype(BF16)
            w_out = jnp.concatenate([w_out_a[e][:ATTN_WIDTH][head_cols], w_out_a[e][ATTN_WIDTH:]], axis=0).astype(BF16)
            gq = jnp.tile(qk_norm_g[e, 0], LANES // HEAD_DIM)[None]
            gk = jnp.tile(qk_norm_g[e, 1], LANES // HEAD_DIM)[None]
            qt, k, vt, p = _project_a(xs, mods, norm_g[i, 1][None], w_in, gq, gk, head_mean,
                                      rope_by_row, rope_by_col, n_tiles=n1, lat_tiles=lat_tiles,
                                      tiles_per_batch=tiles_per_batch, **common)
            attn = _attention(qt, k, vt, batch=batch, seq=seq, ctx_len=ctx_len, latent=True)
            attn_c = (_attention(qt, k, vt, batch=batch, seq=seq, ctx_len=ctx_len, latent=False)
                      if later_reads_ctx else attn)
            xs = _combine_a(xs, mods, attn, attn_c, p, pool_w[e].astype(BF16), pool_scale[e][None], w_out,
                            n_tiles=n2, lat_rows=lat_rows, seq=seq, ctx_len=ctx_len, **common)
        else:
            o = i // 2
            xs = _mixer_c(xs, mods, norm_g[i, 1][None], w_in_c[o].astype(BF16), v_norm_g[o][None],
                          w_sp[o].astype(BF16), b_sp[o].T, w_out_c[o].astype(BF16), n_tiles=n2, **common)

        xs = _half_ffn(xs, mods, norm_g[i, 2][None], ffn_w13, ffn_w2, layer=i, half=1, n_tiles=n2, **common)
    return xs[:lat_rows].reshape(batch, seq, d)
```

```python
import functools

import jax
import jax.numpy as jnp
import numpy as np
from jax import lax
from jax.experimental import pallas as pl
from jax.experimental.pallas import tpu as pltpu

F32 = jnp.float32
BF16 = jnp.bfloat16

GRID_W = 64
HEAD_DIM = 64
N_Q_HEADS = 8
N_KV_HEADS = 2
Q_GROUP = N_Q_HEADS // N_KV_HEADS
ATTN_WIDTH = N_Q_HEADS * HEAD_DIM
KV_WIDTH = N_KV_HEADS * HEAD_DIM
ATTN_SCALE = HEAD_DIM ** -0.5
Q_SCALE_LOG2 = float(ATTN_SCALE * np.log2(np.e))
ROPE_THETA = 10000.0
POOL_WINDOWS = (2, 4, 8, 16)
POOL_GROUP_DIM = 128
CHUNK = 128
GMLP_GROUPS = 8
N_MOD = 9
EPS = 1e-6

LANES = 128
SUBLANES = 8
VMEM_LIMIT_BYTES = 56 * 1024 * 1024

TM = 512
FF_CHUNK = 256
TQ = 128
TK = 512
HALO = 8


def _cparams(n_axes):
    return pltpu.CompilerParams(
        dimension_semantics=("arbitrary",) * n_axes,
        vmem_limit_bytes=VMEM_LIMIT_BYTES)


def _resident(shape):
    zeros = (0,) * len(shape)
    return pl.BlockSpec(shape, lambda *_: zeros, pipeline_mode=pl.Buffered(1))


def _modulate(x, g, shift, scale):
    ms = jnp.mean(x * x, axis=-1, keepdims=True)
    return x * lax.rsqrt(ms + EPS) * (g * (1.0 + scale)) + shift


def _mod_kernel(c_ref, w_ref, b_ref, o_ref):
    c = c_ref[...]
    s = (c * jax.nn.sigmoid(c)).astype(BF16)
    o_ref[0, 0] = jnp.dot(s, w_ref[0].astype(BF16), preferred_element_type=F32) + b_ref[0, 0]


def _modulation(c_rows, w_mod, b_mod):
    depth, d, _ = w_mod.shape
    rows = c_rows.shape[0]
    out = pl.pallas_call(
        _mod_kernel,
        grid=(depth, N_MOD),
        in_specs=[
            pl.BlockSpec((rows, d), lambda i, j: (0, 0)),
            pl.BlockSpec((1, d, d), lambda i, j: (i, 0, j)),
            pl.BlockSpec((1, 1, 1, d), lambda i, j: (i, j, 0, 0)),
        ],
        out_specs=pl.BlockSpec((1, 1, rows, d), lambda i, j: (i, j, 0, 0)),
        out_shape=jax.ShapeDtypeStruct((depth, N_MOD, rows, d), F32),
        compiler_params=_cparams(2),
        name="adaln_modulation",
    )(c_rows, w_mod, b_mod.reshape(depth, N_MOD, 1, d))
    return out


def _ffn_kernel(*refs, k0, head_tiles, layer, half):
    if head_tiles is None:
        x_ref, mod_ref, g_ref, w13_hbm, w2_hbm, o_ref, w13_ref, w2_ref, sem, h_ref, act_ref, acc_ref = refs
        x = x_ref[...]
    else:
        (x_ref, tail_ref, mod_ref, g_ref, w13_hbm, w2_hbm, o_ref, w13_ref, w2_ref, sem,
         h_ref, act_ref, acc_ref) = refs
        x = jnp.where(pl.program_id(0) < head_tiles, x_ref[...], tail_ref[...])
    mod = mod_ref[0]
    d_ff = w2_ref.shape[0]
    n_chunks = d_ff // FF_CHUNK
    first_step = pl.program_id(0) == 0

    def chunk(c, offset=0):
        start = c * FF_CHUNK + offset
        return pl.ds(start if isinstance(c, int) else pl.multiple_of(start, LANES), FF_CHUNK)

    def up_copy(c, part):
        cols = chunk(c, part * d_ff)
        return pltpu.make_async_copy(w13_hbm.at[layer, half, :, cols], w13_ref.at[:, cols], sem.at[part, c])

    def down_copy(c):
        return pltpu.make_async_copy(w2_hbm.at[layer, half, chunk(c), :], w2_ref.at[chunk(c), :], sem.at[2, c])

    def half_step(weights_in_flight):
        h_ref[...] = _modulate(x, g_ref[...], mod[k0:k0 + 1], mod[k0 + 1:k0 + 2]).astype(BF16)

        def gate(slot, c):
            if weights_in_flight:
                up_copy(c, 0).wait()
                up_copy(c, 1).wait()
            h = h_ref[...]
            a = jnp.dot(h, w13_ref[:, chunk(c)].astype(BF16), preferred_element_type=F32)
            b = jnp.dot(h, w13_ref[:, chunk(c, d_ff)].astype(BF16), preferred_element_type=F32)
            act_ref[slot] = (a * jax.nn.sigmoid(a) * b).astype(BF16)

        def down(slot, c, first=False):
            if weights_in_flight:
                down_copy(c).wait()
            part = jnp.dot(act_ref[slot], w2_ref[chunk(c), :].astype(BF16), preferred_element_type=F32)
            acc_ref[...] = part if first else acc_ref[...] + part

        assert n_chunks >= 2
        gate(0, 0)
        gate(1, 1)
        down(0, 0, first=True)

        def body(j, carry):
            c = 2 * j
            gate(0, c + 2)
            down(1, c + 1)
            gate(1, c + 3)
            down(0, c + 2)
            return carry

        lax.fori_loop(0, (n_chunks - 2) // 2, body, 0)
        if n_chunks % 2:
            gate(0, n_chunks - 1)
            down(1, n_chunks - 2)
            down(0, n_chunks - 1)
        else:
            down(1, n_chunks - 1)
        o_ref[...] = x + 0.5 * mod[k0 + 2:k0 + 3] * acc_ref[...]

    @pl.when(first_step)
    def _():
        for c in range(n_chunks):
            up_copy(c, 0).start()
            up_copy(c, 1).start()
            down_copy(c).start()
        half_step(weights_in_flight=True)

    @pl.when(jnp.logical_not(first_step))
    def _():
        half_step(weights_in_flight=False)


def _half_ffn(xs, mods, g, w13, w2, *, layer, half, n_tiles, group_of_tile, tail=None):
    d = xs.shape[1]
    d_ff = w2.shape[2]
    k0 = 6 * half
    in_hbm = pl.BlockSpec(memory_space=pl.ANY)
    if tail is None:
        head_tiles = None
        stream, stream_specs = [xs], [pl.BlockSpec((TM, d), lambda t: (t, 0))]
    else:
        head_tiles = xs.shape[0] // TM
        stream = [xs, tail]
        stream_specs = [pl.BlockSpec((TM, d), lambda t: (jnp.minimum(t, head_tiles - 1), 0)),
                        pl.BlockSpec((TM, d), lambda t: (jnp.maximum(t - head_tiles, 0), 0))]
    return pl.pallas_call(
        functools.partial(_ffn_kernel, k0=k0, head_tiles=head_tiles, layer=layer, half=half),
        grid=(n_tiles,),
        in_specs=stream_specs + [
            pl.BlockSpec((1, N_MOD, d), lambda t: (group_of_tile(t), 0, 0)),
            pl.BlockSpec((1, d), lambda t: (0, 0)),
            in_hbm,
            in_hbm,
        ],
        out_specs=pl.BlockSpec((TM, d), lambda t: (t, 0)),
        out_shape=jax.ShapeDtypeStruct((n_tiles * TM, d), F32),
        scratch_shapes=[pltpu.VMEM((d, 2 * d_ff), F32), pltpu.VMEM((d_ff, d), F32),
                        pltpu.SemaphoreType.DMA((3, d_ff // FF_CHUNK)),
                        pltpu.VMEM((TM, d), BF16), pltpu.VMEM((2, TM, FF_CHUNK), BF16),
                        pltpu.VMEM((TM, d), F32)],
        compiler_params=_cparams(1),
        name="half_ffn",
    )(*stream, mods, g, w13, w2)


def _rope(x, cos, sin_hi, sin_lo):
    return (x * cos + pltpu.roll(x, 16, axis=1) * sin_hi
            + pltpu.roll(x, LANES - 16, axis=1) * sin_lo)


def _proj_a_kernel(x_ref, mod_ref, g_ref, w_ref, gq_ref, gk_ref, hsum_ref,
                   rope_row_ref, rope_col_ref, qt_ref, k_ref, vt_ref, p_ref, *, lat_tiles):
    mod = mod_ref[0]
    h = _modulate(x_ref[...], g_ref[...], mod[3:4], mod[4:5]).astype(BF16)
    z = jnp.dot(h, w_ref[...], preferred_element_type=F32)

    by_row, by_col = rope_row_ref[...], rope_col_ref[...]
    rope = jnp.concatenate([jnp.broadcast_to(by_row[r:r + 1, :], by_col.shape) + by_col
                            for r in range(TM // GRID_W)], axis=0)
    is_ctx = pl.program_id(0) >= lat_tiles
    cos = jnp.where(is_ctx, 1.0, rope[:, :LANES])
    shi = jnp.where(is_ctx, 0.0, rope[:, LANES:2 * LANES])
    slo = jnp.where(is_ctx, 0.0, rope[:, 2 * LANES:])
    hsum = hsum_ref[...]

    def norm_rope(zb, gain):
        ms = jnp.dot((zb * zb).astype(BF16), hsum, preferred_element_type=F32)
        return _rope(zb * lax.rsqrt(ms + EPS) * gain, cos, shi, slo)

    for j in range(ATTN_WIDTH // LANES):
        qb = norm_rope(z[:, j * LANES:(j + 1) * LANES], gq_ref[...])
        qt_ref[j * LANES:(j + 1) * LANES, :] = (qb * Q_SCALE_LOG2).T.astype(BF16)
    k_ref[...] = norm_rope(z[:, ATTN_WIDTH:ATTN_WIDTH + KV_WIDTH], gk_ref[...]).astype(BF16)
    vt_ref[0] = z[:, ATTN_WIDTH + KV_WIDTH:ATTN_WIDTH + 2 * KV_WIDTH].T.astype(BF16)
    p_ref[...] = z[:, ATTN_WIDTH + 2 * KV_WIDTH:]


def _project_a(xs, mods, g, w_in, gq, gk, hsum, rope_by_row, rope_by_col, *, n_tiles, group_of_tile,
               lat_tiles, tiles_per_batch):
    d = xs.shape[1]
    nt = n_tiles * TM
    grid_rows = TM // GRID_W
    pool_width = w_in.shape[1] - ATTN_WIDTH - 2 * KV_WIDTH
    row_tile = lambda w: pl.BlockSpec((TM, w), lambda t: (t, 0))
    const2 = lambda shape: pl.BlockSpec(shape, lambda t: (0, 0))
    return pl.pallas_call(
        functools.partial(_proj_a_kernel, lat_tiles=lat_tiles),
        grid=(n_tiles,),
        in_specs=[
            row_tile(d),
            pl.BlockSpec((1, N_MOD, d), lambda t: (group_of_tile(t), 0, 0)),
            const2((1, d)),
            _resident(w_in.shape),
            const2((1, LANES)), const2((1, LANES)), const2((LANES, LANES)),
            pl.BlockSpec((grid_rows, 3 * LANES), lambda t: (t % tiles_per_batch, 0)),
            const2(rope_by_col.shape),
        ],
        out_specs=[
            pl.BlockSpec((ATTN_WIDTH, TM), lambda t: (0, t)),
            row_tile(KV_WIDTH),
            pl.BlockSpec((1, KV_WIDTH, TM), lambda t: (t, 0, 0)),
            row_tile(pool_width),
        ],
        out_shape=[
            jax.ShapeDtypeStruct((ATTN_WIDTH, nt), BF16),
            jax.ShapeDtypeStruct((nt, KV_WIDTH), BF16),
            jax.ShapeDtypeStruct((n_tiles, KV_WIDTH, TM), BF16),
            jax.ShapeDtypeStruct((nt, pool_width), F32),
        ],
        compiler_params=_cparams(1),
        name="mixer_a_project",
    )(xs, mods, g, w_in, gq, gk, hsum, rope_by_row, rope_by_col)


def _attn_kernel(*refs, n_lat_steps):
    if n_lat_steps:
        qt_ref, kc_ref, vtc_ref, kl_ref, vtl_ref, o_ref, m_ref, l_ref, acc_ref, s_ref = refs
    else:
        qt_ref, kc_ref, vtc_ref, o_ref, m_ref, l_ref, acc_ref = refs
    tq = qt_ref.shape[1]
    n_blocks = ATTN_WIDTH // LANES
    top = lax.broadcasted_iota(jnp.int32, (LANES, tq), 0) < HEAD_DIM
    pieces = []
    for j in range(n_blocks):
        blk = qt_ref[j * LANES:(j + 1) * LANES, :]
        zero = jnp.zeros_like(blk)
        pieces += [jnp.where(top, blk, zero), jnp.where(top, zero, blk)]
    qqt = jnp.concatenate(pieces, axis=1)

    m_ref[...] = jnp.full_like(m_ref, -jnp.inf)
    l_ref[...] = jnp.zeros_like(l_ref)
    acc_ref[...] = jnp.zeros_like(acc_ref)

    def scores(k):
        return jnp.dot(k, qqt, preferred_element_type=F32)

    def accumulate(s, vt):
        m_old = m_ref[...]
        m_new = jnp.maximum(m_old, jnp.max(s, axis=0, keepdims=True))
        alpha = jnp.exp2(m_old - m_new)
        p = jnp.exp2(s - m_new)
        l_ref[...] = alpha * l_ref[...] + jnp.sum(p, axis=0, keepdims=True)
        acc_ref[...] = alpha * acc_ref[...] + jnp.dot(vt, p.astype(BF16), preferred_element_type=F32)
        m_ref[...] = m_new

    def lat_keys(i):
        return kl_ref[pl.ds(pl.multiple_of(i * TK, TK), TK), :]

    s_ctx = scores(kc_ref[...])
    if n_lat_steps:
        s_ref[1] = scores(lat_keys(0))
        accumulate(s_ctx, vtc_ref[0])

        def body(j, carry):
            i = 2 * j
            s_ref[0] = scores(lat_keys(i + 1))
            accumulate(s_ref[1], vtl_ref[i])
            s_ref[1] = scores(lat_keys(i + 2))
            accumulate(s_ref[0], vtl_ref[i + 1])
            return carry
        lax.fori_loop(0, n_lat_steps // 2 - 1, body, 0)
        s_ref[0] = scores(lat_keys(n_lat_steps - 1))
        accumulate(s_ref[1], vtl_ref[n_lat_steps - 2])
        accumulate(s_ref[0], vtl_ref[n_lat_steps - 1])
    else:
        accumulate(s_ctx, vtc_ref[0])

    out_t = acc_ref[...] / l_ref[...]
    for j in range(n_blocks):
        o_top = out_t[:, (2 * j) * tq:(2 * j + 1) * tq]
        o_bottom = out_t[:, (2 * j + 1) * tq:(2 * j + 2) * tq]
        o_ref[:, j * LANES:(j + 1) * LANES] = jnp.where(top, o_top, o_bottom).T.astype(o_ref.dtype)


def _attention(qt, k, vt, *, batch, seq, ctx_len, latent):
    lat_rows = batch * seq
    n_q = (seq if latent else ctx_len) // TQ
    q_blk0 = 0 if latent else lat_rows // TQ
    ctx_per_slab = TM // ctx_len
    ctx_slab0 = lat_rows // TM
    in_specs = [
        pl.BlockSpec((ATTN_WIDTH, TQ), lambda b, i: (0, q_blk0 + b * n_q + i)),
        pl.BlockSpec((ctx_len, KV_WIDTH), lambda b, i: (lat_rows // ctx_len + b, 0)),
        pl.BlockSpec((1, KV_WIDTH, ctx_len),
                     lambda b, i: (ctx_slab0 + b // ctx_per_slab, 0, b % ctx_per_slab)),
    ]
    operands = [qt, k, vt]
    if latent:
        in_specs += [pl.BlockSpec((seq, KV_WIDTH), lambda b, i: (b, 0)),
                     pl.BlockSpec((seq // TM, KV_WIDTH, TM), lambda b, i: (b, 0, 0))]
        operands += [k, vt]
    cols = N_Q_HEADS * TQ
    scratch = [pltpu.VMEM((1, cols), F32), pltpu.VMEM((1, cols), F32), pltpu.VMEM((KV_WIDTH, cols), F32)]
    if latent:
        assert (seq // TK) % 2 == 0 and seq // TK >= 2
        scratch.append(pltpu.VMEM((2, TK, cols), F32))
    return pl.pallas_call(
        functools.partial(_attn_kernel, n_lat_steps=seq // TK if latent else 0),
        grid=(batch, n_q),
        in_specs=in_specs,
        out_specs=pl.BlockSpec((TQ, ATTN_WIDTH), lambda b, i: (b * n_q + i, 0)),
        out_shape=jax.ShapeDtypeStruct((batch * n_q * TQ, ATTN_WIDTH), BF16),
        scratch_shapes=scratch,
        compiler_params=_cparams(2),
        name="gqa_attention_latent" if latent else "gqa_attention_ctx",
    )(*operands)


def _shift_rows(a, d):
    return pltpu.roll(a, d % a.shape[0], axis=0)


def _pool_group(pe, w, edge):
    def before(a, span):
        shifted = _shift_rows(a, span)
        return shifted if edge is None else jnp.where(edge[0] >= span, shifted, jnp.zeros_like(a))

    def after(a, span):
        shifted = _shift_rows(a, -span)
        return shifted if edge is None else jnp.where(edge[0] + span < edge[1], shifted, jnp.zeros_like(a))

    left = before(pe, 1)
    right = pe
    span = 1
    while 2 * span <= w // 2:
        left = left + before(left, span)
        right = right + after(right, span)
        span *= 2
    if edge is None:
        return (left + right) * (1.0 / w)
    pos, n = edge
    count = jnp.maximum(jnp.minimum(pos + w // 2, n) - jnp.maximum(pos - w // 2, 0), 1)
    return (left + right) / count.astype(F32)


def _rem_static(r, m):
    return r & (m - 1) if m & (m - 1) == 0 else lax.rem(r, m)


def _comb_a_kernel(x_ref, mod_ref, attn_ref, attn_tail_ref, p_ref, pprev_ref, pnext_ref, pw_ref, ps_ref,
                   wo_ref, o_ref, pooled_ref, *, lat_rows, seq, ctx_len, attn_head_tiles):
    tile = pl.program_id(0)
    tiles_per_seq = seq // TM

    def mix(edge):
        attn = jnp.where(tile < attn_head_tiles, attn_ref[...], attn_tail_ref[...])
        mixed = jnp.dot(attn, wo_ref[:ATTN_WIDTH, :], preferred_element_type=F32)
        for gi, w in enumerate(POOL_WINDOWS):
            cols = slice(gi * POOL_GROUP_DIM, (gi + 1) * POOL_GROUP_DIM)
            pe = jnp.concatenate([pprev_ref[:, cols], p_ref[:, cols], pnext_ref[:, cols]], axis=0)
            centred = (_pool_group(pe, w, edge) - pe)[HALO:HALO + TM]
            pooled = jnp.dot(centred.astype(BF16), pw_ref[gi], preferred_element_type=F32) * ps_ref[:, cols]
            pooled_ref[:, cols] = pooled.astype(BF16)
        mixed += jnp.dot(pooled_ref[...], wo_ref[ATTN_WIDTH:, :], preferred_element_type=F32)
        o_ref[...] = x_ref[...] + mod_ref[0][5:6] * mixed

    seq_tile = lax.rem(tile, tiles_per_seq)
    at_edge = (tile >= lat_rows // TM) | (seq_tile == 0) | (seq_tile == tiles_per_seq - 1)

    @pl.when(at_edge)
    def _():
        rows = TM + 2 * HALO
        r = tile * TM - HALO + lax.broadcasted_iota(jnp.int32, (rows, POOL_GROUP_DIM), 0)
        is_lat = r < lat_rows
        n = jnp.where(is_lat, seq, ctx_len)
        pos = jnp.where(is_lat, _rem_static(r, seq), _rem_static(r - lat_rows, ctx_len))
        mix((pos, n))

    @pl.when(jnp.logical_not(at_edge))
    def _():
        mix(None)


def _combine_a(xs, mods, attn, attn_tail, p, pool_w, pool_scale, w_out, *, n_tiles, group_of_tile,
               lat_rows, seq, ctx_len):
    d = xs.shape[1]
    pool_width = p.shape[1]
    halo_per_tile = TM // HALO
    last_halo_blk = p.shape[0] // HALO - 1
    head_tiles = attn.shape[0] // TM
    return pl.pallas_call(
        functools.partial(_comb_a_kernel, lat_rows=lat_rows, seq=seq, ctx_len=ctx_len,
                          attn_head_tiles=head_tiles),
        grid=(n_tiles,),
        in_specs=[
            pl.BlockSpec((TM, d), lambda t: (t, 0)),
            pl.BlockSpec((1, N_MOD, d), lambda t: (group_of_tile(t), 0, 0)),
            pl.BlockSpec((TM, ATTN_WIDTH), lambda t: (jnp.minimum(t, head_tiles - 1), 0)),
            pl.BlockSpec((TM, ATTN_WIDTH), lambda t: (jnp.maximum(t - head_tiles, 0), 0)),
            pl.BlockSpec((TM, pool_width), lambda t: (t, 0)),
            pl.BlockSpec((HALO, pool_width), lambda t: (jnp.maximum(t * halo_per_tile - 1, 0), 0)),
            pl.BlockSpec((HALO, pool_width),
                         lambda t: (jnp.minimum((t + 1) * halo_per_tile, last_halo_blk), 0)),
            _resident(pool_w.shape),
            pl.BlockSpec((1, pool_width), lambda t: (0, 0)),
            _resident(w_out.shape),
        ],
        out_specs=pl.BlockSpec((TM, d), lambda t: (t, 0)),
        out_shape=jax.ShapeDtypeStruct((n_tiles * TM, d), F32),
        scratch_shapes=[pltpu.VMEM((TM, pool_width), BF16)],
        compiler_params=_cparams(1),
        name="mixer_a_combine",
    )(xs, mods, attn, attn_tail, p, p, p, pool_w, pool_scale, w_out)


def _gelu_tanh(x):
    return 0.5 * x * (1.0 + jnp.tanh(np.sqrt(2.0 / np.pi).astype(np.float32) * (x + 0.044715 * (x * x * x))))


def _mixer_c_kernel(x_ref, mod_ref, g_ref, wi_ref, vg_ref, wsp_ref, bsp_ref, wo_ref, o_ref, gated_ref):
    x = x_ref[...]
    mod = mod_ref[0]
    width = wo_ref.shape[0]
    h = _modulate(x, g_ref[...], mod[3:4], mod[4:5]).astype(BF16)
    z = _gelu_tanh(jnp.dot(h, wi_ref[...], preferred_element_type=F32))
    u = z[:, :width]
    v = z[:, width:]
    v = (v * lax.rsqrt(jnp.mean(v * v, axis=-1, keepdims=True) + EPS) * vg_ref[...]).astype(BF16)
    bias = bsp_ref[...]
    for g in range(GMLP_GROUPS):
        cols = slice(g * CHUNK, (g + 1) * CHUNK)
        b_col = bias[:, g:g + 1]
        for c in range(TM // CHUNK):
            rows = slice(c * CHUNK, (c + 1) * CHUNK)
            sv = jnp.dot(wsp_ref[g], v[rows, cols], preferred_element_type=F32) + b_col
            gated_ref[rows, cols] = (u[rows, cols] * sv).astype(BF16)
    o_ref[...] = x + mod[5:6] * jnp.dot(gated_ref[...], wo_ref[...], preferred_element_type=F32)


def _mixer_c(xs, mods, g, w_in, v_g, w_sp, b_sp_cols, w_out, *, n_tiles, group_of_tile):
    d = xs.shape[1]
    width = w_out.shape[0]
    return pl.pallas_call(
        _mixer_c_kernel,
        grid=(n_tiles,),
        in_specs=[
            pl.BlockSpec((TM, d), lambda t: (t, 0)),
            pl.BlockSpec((1, N_MOD, d), lambda t: (group_of_tile(t), 0, 0)),
            pl.BlockSpec((1, d), lambda t: (0, 0)),
            _resident(w_in.shape),
            pl.BlockSpec((1, width), lambda t: (0, 0)),
            _resident(w_sp.shape),
            pl.BlockSpec(b_sp_cols.shape, lambda t: (0, 0)),
            _resident(w_out.shape),
        ],
        out_specs=pl.BlockSpec((TM, d), lambda t: (t, 0)),
        out_shape=jax.ShapeDtypeStruct((n_tiles * TM, d), F32),
        scratch_shapes=[pltpu.VMEM((TM, width), BF16)],
        compiler_params=_cparams(1),
        name="mixer_c_gmlp",
    )(xs, mods, g, w_in, v_g, w_sp, b_sp_cols, w_out)


def _rope_tables(seq):
    half = HEAD_DIM // 2
    inv_freq = ROPE_THETA ** (-jnp.arange(0, half, 2, dtype=F32) / half)

    def parts(n, row_axis):
        ang = jnp.arange(n).astype(F32)[:, None] * inv_freq[None, :]
        cos, sin, zero = jnp.cos(ang), jnp.sin(ang), jnp.zeros_like(ang)
        if row_axis:
            heads = ([cos, cos, zero, zero], [zero, sin, zero, zero], [-sin, zero, zero, zero])
        else:
            heads = ([zero, zero, cos, cos], [zero, zero, zero, sin], [zero, zero, -sin, zero])
        return jnp.concatenate([blk for head in heads for blk in head * (LANES // HEAD_DIM)], axis=1)

    return parts(seq // GRID_W, True), parts(GRID_W, False)


def _head_block_order():
    order = []
    for j in range(Q_GROUP):
        order += [j, j + Q_GROUP]
    return np.concatenate([np.arange(h * HEAD_DIM, (h + 1) * HEAD_DIM) for h in order])


def kernel(x, c, ctx, c_ctx, w_mod, b_mod, norm_g, ffn_w13, ffn_w2, w_in_a, qk_norm_g, pool_w,
           pool_scale, w_out_a, w_in_c, v_norm_g, w_sp, b_sp, w_out_c):
    batch, seq, d = x.shape
    ctx_len = ctx.shape[1]
    depth = w_mod.shape[0]
    d_ff = ffn_w2.shape[2]
    lat_rows = batch * seq
    ctx_rows = batch * ctx_len
    assert seq % TM == 0 and ctx_rows % TM == 0 and seq % TK == 0 and ctx_len % TQ == 0
    assert ctx_len % HALO == 0 and lat_rows % ctx_len == 0 and d_ff % FF_CHUNK == 0
    assert TK == TM and TM % ctx_len == 0
    assert seq % GRID_W == 0 and TM % (GRID_W * SUBLANES) == 0
    lat_tiles = lat_rows // TM
    all_tiles = lat_tiles + ctx_rows // TM
    tiles_per_batch = seq // TM
    group_of_tile = lambda t: jnp.minimum(t // tiles_per_batch, batch)

    c_rows = jnp.concatenate([c, c_ctx[None, :], jnp.zeros((SUBLANES - batch - 1, d), F32)], axis=0)
    mods_all = _modulation(c_rows, w_mod, b_mod)
    mods_all = jnp.transpose(mods_all[:, :, :batch + 1], (0, 2, 1, 3))

    head_cols = _head_block_order()
    rope_by_row, rope_by_col = _rope_tables(seq)
    head_mean = jnp.asarray(np.kron(np.eye(LANES // HEAD_DIM), np.full((HEAD_DIM, HEAD_DIM), 1.0 / HEAD_DIM)), BF16)

    xs = x.reshape(lat_rows, d)
    tail = ctx.reshape(ctx_rows, d)
    for i in range(depth):
        mods = mods_all[i]
        reads_ctx = i % 2 == 0
        later_reads_ctx = any(j % 2 == 0 for j in range(i + 1, depth))
        n1 = all_tiles if (reads_ctx or later_reads_ctx) else lat_tiles
        n2 = all_tiles if later_reads_ctx else lat_tiles
        common = dict(group_of_tile=group_of_tile)

        xs = _half_ffn(xs, mods, norm_g[i, 0][None], ffn_w13, ffn_w2, layer=i, half=0, n_tiles=n1,
                       tail=tail if i == 0 else None, **common)

        if reads_ctx:
            e = i // 2
            w_in = w_in_a[e]
            w_in = jnp.concatenate([w_in[:, :ATTN_WIDTH][:, head_cols], w_in[:, ATTN_WIDTH:]], axis=1).astype(BF16)
            w_out = jnp.concatenate([w_out_a[e][:ATTN_WIDTH][head_cols], w_out_a[e][ATTN_WIDTH:]], axis=0).astype(BF16)
            gq = jnp.tile(qk_norm_g[e, 0], LANES // HEAD_DIM)[None]
            gk = jnp.tile(qk_norm_g[e, 1], LANES // HEAD_DIM)[None]
            qt, k, vt, p = _project_a(xs, mods, norm_g[i, 1][None], w_in, gq, gk, head_mean,
                                      rope_by_row, rope_by_col, n_tiles=n1, lat_tiles=lat_tiles,
                                      tiles_per_batch=tiles_per_batch, **common)
            attn = _attention(qt, k, vt, batch=batch, seq=seq, ctx_len=ctx_len, latent=True)
            attn_c = (_attention(qt, k, vt, batch=batch, seq=seq, ctx_len=ctx_len, latent=False)
                      if later_reads_ctx else attn)
            xs = _combine_a(xs, mods, attn, attn_c, p, pool_w[e].astype(BF16), pool_scale[e][None], w_out,
                            n_tiles=n2, lat_rows=lat_rows, seq=seq, ctx_len=ctx_len, **common)
        else:
            o = i // 2
            xs = _mixer_c(xs, mods, norm_g[i, 1][None], w_in_c[o].astype(BF16), v_norm_g[o][None],
                          w_sp[o].astype(BF16), b_sp[o].T, w_out_c[o].astype(BF16), n_tiles=n2, **common)

        xs = _half_ffn(xs, mods, norm_g[i, 2][None], ffn_w13, ffn_w2, layer=i, half=1, n_tiles=n2, **common)
    return xs[:lat_rows].reshape(batch, seq, d)
```

```python
import functools

import jax
import jax.numpy as jnp
import numpy as np
from jax import lax
from jax.experimental import pallas as pl
from jax.experimental.pallas import tpu as pltpu

F32 = jnp.float32
BF16 = jnp.bfloat16

GRID_W = 64
HEAD_DIM = 64
N_Q_HEADS = 8
N_KV_HEADS = 2
Q_GROUP = N_Q_HEADS // N_KV_HEADS
ATTN_WIDTH = N_Q_HEADS * HEAD_DIM
KV_WIDTH = N_KV_HEADS * HEAD_DIM
ATTN_SCALE = HEAD_DIM ** -0.5
Q_SCALE_LOG2 = float(ATTN_SCALE * np.log2(np.e))
ROPE_THETA = 10000.0
POOL_WINDOWS = (2, 4, 8, 16)
POOL_GROUP_DIM = 128
CHUNK = 128
GMLP_GROUPS = 8
N_MOD = 9
EPS = 1e-6

LANES = 128
SUBLANES = 8
VMEM_LIMIT_BYTES = 56 * 1024 * 1024

TM = 512
FF_CHUNK = 256
TQ = 128
TK = 512
HALO = 8


def _cparams(n_axes):
    return pltpu.CompilerParams(
        dimension_semantics=("arbitrary",) * n_axes,
        vmem_limit_bytes=VMEM_LIMIT_BYTES)


def _resident(shape):
    zeros = (0,) * len(shape)
    return pl.BlockSpec(shape, lambda *_: zeros, pipeline_mode=pl.Buffered(1))


def _modulate(x, g, shift, scale):
    ms = jnp.mean(x * x, axis=-1, keepdims=True)
    return x * lax.rsqrt(ms + EPS) * (g * (1.0 + scale)) + shift


def _mod_kernel(c_ref, w_ref, b_ref, o_ref):
    c = c_ref[...]
    s = (c * jax.nn.sigmoid(c)).astype(BF16)
    o_ref[0, 0] = jnp.dot(s, w_ref[0].astype(BF16), preferred_element_type=F32) + b_ref[0, 0]


def _modulation(c_rows, w_mod, b_mod):
    depth, d, _ = w_mod.shape
    rows = c_rows.shape[0]
    out = pl.pallas_call(
        _mod_kernel,
        grid=(depth, N_MOD),
        in_specs=[
            pl.BlockSpec((rows, d), lambda i, j: (0, 0)),
            pl.BlockSpec((1, d, d), lambda i, j: (i, 0, j)),
            pl.BlockSpec((1, 1, 1, d), lambda i, j: (i, j, 0, 0)),
        ],
        out_specs=pl.BlockSpec((1, 1, rows, d), lambda i, j: (i, j, 0, 0)),
        out_shape=jax.ShapeDtypeStruct((depth, N_MOD, rows, d), F32),
        compiler_params=_cparams(2),
        name="adaln_modulation",
    )(c_rows, w_mod, b_mod.reshape(depth, N_MOD, 1, d))
    return out


def _ffn_kernel(*refs, k0, head_tiles, layer, half):
    if head_tiles is None:
        x_ref, mod_ref, g_ref, w13_hbm, w2_hbm, o_ref, w13_ref, w2_ref, sem, h_ref, act_ref, acc_ref = refs
        x = x_ref[...]
    else:
        (x_ref, tail_ref, mod_ref, g_ref, w13_hbm, w2_hbm, o_ref, w13_ref, w2_ref, sem,
         h_ref, act_ref, acc_ref) = refs
        x = jnp.where(pl.program_id(0) < head_tiles, x_ref[...], tail_ref[...])
    mod = mod_ref[0]
    d_ff = w2_ref.shape[0]
    n_chunks = d_ff // FF_CHUNK
    first_step = pl.program_id(0) == 0

    def chunk(c, offset=0):
        start = c * FF_CHUNK + offset
        return pl.ds(start if isinstance(c, int) else pl.multiple_of(start, LANES), FF_CHUNK)

    def up_copy(c, part):
        cols = chunk(c, part * d_ff)
        return pltpu.make_async_copy(w13_hbm.at[layer, half, :, cols], w13_ref.at[:, cols], sem.at[part, c])

    def down_copy(c):
        return pltpu.make_async_copy(w2_hbm.at[layer, half, chunk(c), :], w2_ref.at[chunk(c), :], sem.at[2, c])

    def half_step(weights_in_flight):
        h_ref[...] = _modulate(x, g_ref[...], mod[k0:k0 + 1], mod[k0 + 1:k0 + 2]).astype(BF16)

        def gate(slot, c):
            if weights_in_flight:
                up_copy(c, 0).wait()
                up_copy(c, 1).wait()
            h = h_ref[...]
            a = jnp.dot(h, w13_ref[:, chunk(c)].astype(BF16), preferred_element_type=F32)
            b = jnp.dot(h, w13_ref[:, chunk(c, d_ff)].astype(BF16), preferred_element_type=F32)
            act_ref[slot] = (a * jax.nn.sigmoid(a) * b).astype(BF16)

        def down(slot, c, first=False):
            if weights_in_flight:
                down_copy(c).wait()
            part = jnp.dot(act_ref[slot], w2_ref[chunk(c), :].astype(BF16), preferred_element_type=F32)
            acc_ref[...] = part if first else acc_ref[...] + part

        assert n_chunks >= 2
        gate(0, 0)
        gate(1, 1)
        down(0, 0, first=True)

        def body(j, carry):
            c = 2 * j
            gate(0, c + 2)
            down(1, c + 1)
            gate(1, c + 3)
            down(0, c + 2)
            return carry

        lax.fori_loop(0, (n_chunks - 2) // 2, body, 0, unroll=True)
        if n_chunks % 2:
            gate(0, n_chunks - 1)
            down(1, n_chunks - 2)
            down(0, n_chunks - 1)
        else:
            down(1, n_chunks - 1)
        o_ref[...] = x + 0.5 * mod[k0 + 2:k0 + 3] * acc_ref[...]

    @pl.when(first_step)
    def _():
        for c in range(n_chunks):
            up_copy(c, 0).start()
            up_copy(c, 1).start()
            down_copy(c).start()
        half_step(weights_in_flight=True)

    @pl.when(jnp.logical_not(first_step))
    def _():
        half_step(weights_in_flight=False)


def _half_ffn(xs, mods, g, w13, w2, *, layer, half, n_tiles, group_of_tile, tail=None):
    d = xs.shape[1]
    d_ff = w2.shape[2]
    k0 = 6 * half
    in_hbm = pl.BlockSpec(memory_space=pl.ANY)
    if tail is None:
        head_tiles = None
        stream, stream_specs = [xs], [pl.BlockSpec((TM, d), lambda t: (t, 0))]
    else:
        head_tiles = xs.shape[0] // TM
        stream = [xs, tail]
        stream_specs = [pl.BlockSpec((TM, d), lambda t: (jnp.minimum(t, head_tiles - 1), 0)),
                        pl.BlockSpec((TM, d), lambda t: (jnp.maximum(t - head_tiles, 0), 0))]
    return pl.pallas_call(
        functools.partial(_ffn_kernel, k0=k0, head_tiles=head_tiles, layer=layer, half=half),
        grid=(n_tiles,),
        in_specs=stream_specs + [
            pl.BlockSpec((1, N_MOD, d), lambda t: (group_of_tile(t), 0, 0)),
            pl.BlockSpec((1, d), lambda t: (0, 0)),
            in_hbm,
            in_hbm,
        ],
        out_specs=pl.BlockSpec((TM, d), lambda t: (t, 0)),
        out_shape=jax.ShapeDtypeStruct((n_tiles * TM, d), F32),
        scratch_shapes=[pltpu.VMEM((d, 2 * d_ff), F32), pltpu.VMEM((d_ff, d), F32),
                        pltpu.SemaphoreType.DMA((3, d_ff // FF_CHUNK)),
                        pltpu.VMEM((TM, d), BF16), pltpu.VMEM((2, TM, FF_CHUNK), BF16),
                        pltpu.VMEM((TM, d), F32)],
        compiler_params=_cparams(1),
        name="half_ffn",
    )(*stream, mods, g, w13, w2)


def _rope(x, cos, sin_hi, sin_lo):
    return (x * cos + pltpu.roll(x, 16, axis=1) * sin_hi
            + pltpu.roll(x, LANES - 16, axis=1) * sin_lo)


def _proj_a_kernel(x_ref, mod_ref, g_ref, w_ref, gq_ref, gk_ref, hsum_ref,
                   rope_row_ref, rope_col_ref, qt_ref, k_ref, vt_ref, p_ref, *, lat_tiles):
    mod = mod_ref[0]
    h = _modulate(x_ref[...], g_ref[...], mod[3:4], mod[4:5]).astype(BF16)
    z = jnp.dot(h, w_ref[...], preferred_element_type=F32)

    by_row, by_col = rope_row_ref[...], rope_col_ref[...]
    rope = jnp.concatenate([jnp.broadcast_to(by_row[r:r + 1, :], by_col.shape) + by_col
                            for r in range(TM // GRID_W)], axis=0)
    is_ctx = pl.program_id(0) >= lat_tiles
    cos = jnp.where(is_ctx, 1.0, rope[:, :LANES])
    shi = jnp.where(is_ctx, 0.0, rope[:, LANES:2 * LANES])
    slo = jnp.where(is_ctx, 0.0, rope[:, 2 * LANES:])
    hsum = hsum_ref[...]

    def norm_rope(zb, gain):
        ms = jnp.dot((zb * zb).astype(BF16), hsum, preferred_element_type=F32)
        return _rope(zb * lax.rsqrt(ms + EPS) * gain, cos, shi, slo)

    for j in range(ATTN_WIDTH // LANES):
        qb = norm_rope(z[:, j * LANES:(j + 1) * LANES], gq_ref[...])
        qt_ref[j * LANES:(j + 1) * LANES, :] = (qb * Q_SCALE_LOG2).T.astype(BF16)
    k_ref[...] = norm_rope(z[:, ATTN_WIDTH:ATTN_WIDTH + KV_WIDTH], gk_ref[...]).astype(BF16)
    vt_ref[0] = z[:, ATTN_WIDTH + KV_WIDTH:ATTN_WIDTH + 2 * KV_WIDTH].T.astype(BF16)
    p_ref[...] = z[:, ATTN_WIDTH + 2 * KV_WIDTH:]


def _project_a(xs, mods, g, w_in, gq, gk, hsum, rope_by_row, rope_by_col, *, n_tiles, group_of_tile,
               lat_tiles, tiles_per_batch):
    d = xs.shape[1]
    nt = n_tiles * TM
    grid_rows = TM // GRID_W
    pool_width = w_in.shape[1] - ATTN_WIDTH - 2 * KV_WIDTH
    row_tile = lambda w: pl.BlockSpec((TM, w), lambda t: (t, 0))
    const2 = lambda shape: pl.BlockSpec(shape, lambda t: (0, 0))
    return pl.pallas_call(
        functools.partial(_proj_a_kernel, lat_tiles=lat_tiles),
        grid=(n_tiles,),
        in_specs=[
            row_tile(d),
            pl.BlockSpec((1, N_MOD, d), lambda t: (group_of_tile(t), 0, 0)),
            const2((1, d)),
            _resident(w_in.shape),
            const2((1, LANES)), const2((1, LANES)), const2((LANES, LANES)),
            pl.BlockSpec((grid_rows, 3 * LANES), lambda t: (t % tiles_per_batch, 0)),
            const2(rope_by_col.shape),
        ],
        out_specs=[
            pl.BlockSpec((ATTN_WIDTH, TM), lambda t: (0, t)),
            row_tile(KV_WIDTH),
            pl.BlockSpec((1, KV_WIDTH, TM), lambda t: (t, 0, 0)),
            row_tile(pool_width),
        ],
        out_shape=[
            jax.ShapeDtypeStruct((ATTN_WIDTH, nt), BF16),
            jax.ShapeDtypeStruct((nt, KV_WIDTH), BF16),
            jax.ShapeDtypeStruct((n_tiles, KV_WIDTH, TM), BF16),
            jax.ShapeDtypeStruct((nt, pool_width), F32),
        ],
        compiler_params=_cparams(1),
        name="mixer_a_project",
    )(xs, mods, g, w_in, gq, gk, hsum, rope_by_row, rope_by_col)


def _attn_kernel(*refs, n_lat_steps):
    if n_lat_steps:
        qt_ref, kc_ref, vtc_ref, kl_ref, vtl_ref, o_ref, m_ref, l_ref, acc_ref, s_ref = refs
    else:
        qt_ref, kc_ref, vtc_ref, o_ref, m_ref, l_ref, acc_ref = refs
    tq = qt_ref.shape[1]
    n_blocks = ATTN_WIDTH // LANES
    top = lax.broadcasted_iota(jnp.int32, (LANES, tq), 0) < HEAD_DIM
    pieces = []
    for j in range(n_blocks):
        blk = qt_ref[j * LANES:(j + 1) * LANES, :]
        zero = jnp.zeros_like(blk)
        pieces += [jnp.where(top, blk, zero), jnp.where(top, zero, blk)]
    qqt = jnp.concatenate(pieces, axis=1)

    m_ref[...] = jnp.full_like(m_ref, -jnp.inf)
    l_ref[...] = jnp.zeros_like(l_ref)
    acc_ref[...] = jnp.zeros_like(acc_ref)

    def scores(k):
        return jnp.dot(k, qqt, preferred_element_type=F32)

    def accumulate(s, vt):
        m_old = m_ref[...]
        m_new = jnp.maximum(m_old, jnp.max(s, axis=0, keepdims=True))
        alpha = jnp.exp2(m_old - m_new)
        p = jnp.exp2(s - m_new)
        l_ref[...] = alpha * l_ref[...] + jnp.sum(p, axis=0, keepdims=True)
        acc_ref[...] = alpha * acc_ref[...] + jnp.dot(vt, p.astype(BF16), preferred_element_type=F32)
        m_ref[...] = m_new

    def lat_keys(i):
        return kl_ref[pl.ds(pl.multiple_of(i * TK, TK), TK), :]

    s_ctx = scores(kc_ref[...])
    if n_lat_steps:
        s_ref[1] = scores(lat_keys(0))
        accumulate(s_ctx, vtc_ref[0])

        def body(j, carry):
            i = 2 * j
            s_ref[0] = scores(lat_keys(i + 1))
            accumulate(s_ref[1], vtl_ref[i])
            s_ref[1] = scores(lat_keys(i + 2))
            accumulate(s_ref[0], vtl_ref[i + 1])
            return carry
        lax.fori_loop(0, n_lat_steps // 2 - 1, body, 0, unroll=3)
        s_ref[0] = scores(lat_keys(n_lat_steps - 1))
        accumulate(s_ref[1], vtl_ref[n_lat_steps - 2])
        accumulate(s_ref[0], vtl_ref[n_lat_steps - 1])
    else:
        accumulate(s_ctx, vtc_ref[0])

    out_t = acc_ref[...] / l_ref[...]
    for j in range(n_blocks):
        o_top = out_t[:, (2 * j) * tq:(2 * j + 1) * tq]
        o_bottom = out_t[:, (2 * j + 1) * tq:(2 * j + 2) * tq]
        o_ref[:, j * LANES:(j + 1) * LANES] = jnp.where(top, o_top, o_bottom).T.astype(o_ref.dtype)


def _attention(qt, k, vt, *, batch, seq, ctx_len, latent):
    lat_rows = batch * seq
    n_q = (seq if latent else ctx_len) // TQ
    q_blk0 = 0 if latent else lat_rows // TQ
    ctx_per_slab = TM // ctx_len
    ctx_slab0 = lat_rows // TM
    in_specs = [
        pl.BlockSpec((ATTN_WIDTH, TQ), lambda b, i: (0, q_blk0 + b * n_q + i)),
        pl.BlockSpec((ctx_len, KV_WIDTH), lambda b, i: (lat_rows // ctx_len + b, 0)),
        pl.BlockSpec((1, KV_WIDTH, ctx_len),
                     lambda b, i: (ctx_slab0 + b // ctx_per_slab, 0, b % ctx_per_slab)),
    ]
    operands = [qt, k, vt]
    if latent:
        in_specs += [pl.BlockSpec((seq, KV_WIDTH), lambda b, i: (b, 0)),
                     pl.BlockSpec((seq // TM, KV_WIDTH, TM), lambda b, i: (b, 0, 0))]
        operands += [k, vt]
    cols = N_Q_HEADS * TQ
    scratch = [pltpu.VMEM((1, cols), F32), pltpu.VMEM((1, cols), F32), pltpu.VMEM((KV_WIDTH, cols), F32)]
    if latent:
        assert (seq // TK) % 2 == 0 and seq // TK >= 2
        scratch.append(pltpu.VMEM((2, TK, cols), F32))
    return pl.pallas_call(
        functools.partial(_attn_kernel, n_lat_steps=seq // TK if latent else 0),
        grid=(batch, n_q),
        in_specs=in_specs,
        out_specs=pl.BlockSpec((TQ, ATTN_WIDTH), lambda b, i: (b * n_q + i, 0)),
        out_shape=jax.ShapeDtypeStruct((batch * n_q * TQ, ATTN_WIDTH), BF16),
        scratch_shapes=scratch,
        compiler_params=_cparams(2),
        name="gqa_attention_latent" if latent else "gqa_attention_ctx",
    )(*operands)


def _shift_rows(a, d):
    return pltpu.roll(a, d % a.shape[0], axis=0)


def _pool_group(pe, w, edge):
    def before(a, span):
        shifted = _shift_rows(a, span)
        return shifted if edge is None else jnp.where(edge[0] >= span, shifted, jnp.zeros_like(a))

    def after(a, span):
        shifted = _shift_rows(a, -span)
        return shifted if edge is None else jnp.where(edge[0] + span < edge[1], shifted, jnp.zeros_like(a))

    left = before(pe, 1)
    right = pe
    span = 1
    while 2 * span <= w // 2:
        left = left + before(left, span)
        right = right + after(right, span)
        span *= 2
    if edge is None:
        return (left + right) * (1.0 / w)
    pos, n = edge
    count = jnp.maximum(jnp.minimum(pos + w // 2, n) - jnp.maximum(pos - w // 2, 0), 1)
    return (left + right) / count.astype(F32)


def _rem_static(r, m):
    return r & (m - 1) if m & (m - 1) == 0 else lax.rem(r, m)


def _comb_a_kernel(x_ref, mod_ref, attn_ref, attn_tail_ref, p_ref, pprev_ref, pnext_ref, pw_ref, ps_ref,
                   wo_ref, o_ref, pooled_ref, *, lat_rows, seq, ctx_len, attn_head_tiles):
    tile = pl.program_id(0)
    tiles_per_seq = seq // TM

    def mix(edge):
        attn = jnp.where(tile < attn_head_tiles, attn_ref[...], attn_tail_ref[...])
        mixed = jnp.dot(attn, wo_ref[:ATTN_WIDTH, :], preferred_element_type=F32)
        for gi, w in enumerate(POOL_WINDOWS):
            cols = slice(gi * POOL_GROUP_DIM, (gi + 1) * POOL_GROUP_DIM)
            pe = jnp.concatenate([pprev_ref[:, cols], p_ref[:, cols], pnext_ref[:, cols]], axis=0)
            centred = (_pool_group(pe, w, edge) - pe)[HALO:HALO + TM]
            pooled = jnp.dot(centred.astype(BF16), pw_ref[gi], preferred_element_type=F32) * ps_ref[:, cols]
            pooled_ref[:, cols] = pooled.astype(BF16)
        mixed += jnp.dot(pooled_ref[...], wo_ref[ATTN_WIDTH:, :], preferred_element_type=F32)
        o_ref[...] = x_ref[...] + mod_ref[0][5:6] * mixed

    seq_tile = lax.rem(tile, tiles_per_seq)
    at_edge = (tile >= lat_rows // TM) | (seq_tile == 0) | (seq_tile == tiles_per_seq - 1)

    @pl.when(at_edge)
    def _():
        rows = TM + 2 * HALO
        r = tile * TM - HALO + lax.broadcasted_iota(jnp.int32, (rows, POOL_GROUP_DIM), 0)
        is_lat = r < lat_rows
        n = jnp.where(is_lat, seq, ctx_len)
        pos = jnp.where(is_lat, _rem_static(r, seq), _rem_static(r - lat_rows, ctx_len))
        mix((pos, n))

    @pl.when(jnp.logical_not(at_edge))
    def _():
        mix(None)


def _combine_a(xs, mods, attn, attn_tail, p, pool_w, pool_scale, w_out, *, n_tiles, group_of_tile,
               lat_rows, seq, ctx_len):
    d = xs.shape[1]
    pool_width = p.shape[1]
    halo_per_tile = TM // HALO
    last_halo_blk = p.shape[0] // HALO - 1
    head_tiles = attn.shape[0] // TM
    return pl.pallas_call(
        functools.partial(_comb_a_kernel, lat_rows=lat_rows, seq=seq, ctx_len=ctx_len,
                          attn_head_tiles=head_tiles),
        grid=(n_tiles,),
        in_specs=[
            pl.BlockSpec((TM, d), lambda t: (t, 0)),
            pl.BlockSpec((1, N_MOD, d), lambda t: (group_of_tile(t), 0, 0)),
            pl.BlockSpec((TM, ATTN_WIDTH), lambda t: (jnp.minimum(t, head_tiles - 1), 0)),
            pl.BlockSpec((TM, ATTN_WIDTH), lambda t: (jnp.maximum(t - head_tiles, 0), 0)),
            pl.BlockSpec((TM, pool_width), lambda t: (t, 0)),
            pl.BlockSpec((HALO, pool_width), lambda t: (jnp.maximum(t * halo_per_tile - 1, 0), 0)),
            pl.BlockSpec((HALO, pool_width),
                         lambda t: (jnp.minimum((t + 1) * halo_per_tile, last_halo_blk), 0)),
            _resident(pool_w.shape),
            pl.BlockSpec((1, pool_width), lambda t: (0, 0)),
            _resident(w_out.shape),
        ],
        out_specs=pl.BlockSpec((TM, d), lambda t: (t, 0)),
        out_shape=jax.ShapeDtypeStruct((n_tiles * TM, d), F32),
        scratch_shapes=[pltpu.VMEM((TM, pool_width), BF16)],
        compiler_params=_cparams(1),
        name="mixer_a_combine",
    )(xs, mods, attn, attn_tail, p, p, p, pool_w, pool_scale, w_out)


def _gelu_tanh(x):
    return 0.5 * x * (1.0 + jnp.tanh(np.sqrt(2.0 / np.pi).astype(np.float32) * (x + 0.044715 * (x * x * x))))


def _mixer_c_kernel(x_ref, mod_ref, g_ref, wi_ref, vg_ref, wsp_ref, bsp_ref, wo_ref, o_ref, gated_ref):
    x = x_ref[...]
    mod = mod_ref[0]
    width = wo_ref.shape[0]
    h = _modulate(x, g_ref[...], mod[3:4], mod[4:5]).astype(BF16)
    z = _gelu_tanh(jnp.dot(h, wi_ref[...], preferred_element_type=F32))
    u = z[:, :width]
    v = z[:, width:]
    v = (v * lax.rsqrt(jnp.mean(v * v, axis=-1, keepdims=True) + EPS) * vg_ref[...]).astype(BF16)
    bias = bsp_ref[...]
    for g in range(GMLP_GROUPS):
        cols = slice(g * CHUNK, (g + 1) * CHUNK)
        b_col = bias[:, g:g + 1]
        for c in range(TM // CHUNK):
            rows = slice(c * CHUNK, (c + 1) * CHUNK)
            sv = jnp.dot(wsp_ref[g], v[rows, cols], preferred_element_type=F32) + b_col
            gated_ref[rows, cols] = (u[rows, cols] * sv).astype(BF16)
    o_ref[...] = x + mod[5:6] * jnp.dot(gated_ref[...], wo_ref[...], preferred_element_type=F32)


def _mixer_c(xs, mods, g, w_in, v_g, w_sp, b_sp_cols, w_out, *, n_tiles, group_of_tile):
    d = xs.shape[1]
    width = w_out.shape[0]
    return pl.pallas_call(
        _mixer_c_kernel,
        grid=(n_tiles,),
        in_specs=[
            pl.BlockSpec((TM, d), lambda t: (t, 0)),
            pl.BlockSpec((1, N_MOD, d), lambda t: (group_of_tile(t), 0, 0)),
            pl.BlockSpec((1, d), lambda t: (0, 0)),
            _resident(w_in.shape),
            pl.BlockSpec((1, width), lambda t: (0, 0)),
            _resident(w_sp.shape),
            pl.BlockSpec(b_sp_cols.shape, lambda t: (0, 0)),
            _resident(w_out.shape),
        ],
        out_specs=pl.BlockSpec((TM, d), lambda t: (t, 0)),
        out_shape=jax.ShapeDtypeStruct((n_tiles * TM, d), F32),
        scratch_shapes=[pltpu.VMEM((TM, width), BF16)],
        compiler_params=_cparams(1),
        name="mixer_c_gmlp",
    )(xs, mods, g, w_in, v_g, w_sp, b_sp_cols, w_out)


def _rope_tables(seq):
    half = HEAD_DIM // 2
    inv_freq = ROPE_THETA ** (-jnp.arange(0, half, 2, dtype=F32) / half)

    def parts(n, row_axis):
        ang = jnp.arange(n).astype(F32)[:, None] * inv_freq[None, :]
        cos, sin, zero = jnp.cos(ang), jnp.sin(ang), jnp.zeros_like(ang)
        if row_axis:
            heads = ([cos, cos, zero, zero], [zero, sin, zero, zero], [-sin, zero, zero, zero])
        else:
            heads = ([zero, zero, cos, cos], [zero, zero, zero, sin], [zero, zero, -sin, zero])
        return jnp.concatenate([blk for head in heads for blk in head * (LANES // HEAD_DIM)], axis=1)

    return parts(seq // GRID_W, True), parts(GRID_W, False)


def _head_block_order():
    order = []
    for j in range(Q_GROUP):
        order += [j, j + Q_GROUP]
    return np.concatenate([np.arange(h * HEAD_DIM, (h + 1) * HEAD_DIM) for h in order])


def kernel(x, c, ctx, c_ctx, w_mod, b_mod, norm_g, ffn_w13, ffn_w2, w_in_a, qk_norm_g, pool_w,
           pool_scale, w_out_a, w_in_c, v_norm_g, w_sp, b_sp, w_out_c):
    batch, seq, d = x.shape
    ctx_len = ctx.shape[1]
    depth = w_mod.shape[0]
    d_ff = ffn_w2.shape[2]
    lat_rows = batch * seq
    ctx_rows = batch * ctx_len
    assert seq % TM == 0 and ctx_rows % TM == 0 and seq % TK == 0 and ctx_len % TQ == 0
    assert ctx_len % HALO == 0 and lat_rows % ctx_len == 0 and d_ff % FF_CHUNK == 0
    assert TK == TM and TM % ctx_len == 0
    assert seq % GRID_W == 0 and TM % (GRID_W * SUBLANES) == 0
    lat_tiles = lat_rows // TM
    all_tiles = lat_tiles + ctx_rows // TM
    tiles_per_batch = seq // TM
    group_of_tile = lambda t: jnp.minimum(t // tiles_per_batch, batch)

    c_rows = jnp.concatenate([c, c_ctx[None, :], jnp.zeros((SUBLANES - batch - 1, d), F32)], axis=0)
    mods_all = _modulation(c_rows, w_mod, b_mod)
    mods_all = jnp.transpose(mods_all[:, :, :batch + 1], (0, 2, 1, 3))

    head_cols = _head_block_order()
    rope_by_row, rope_by_col = _rope_tables(seq)
    head_mean = jnp.asarray(np.kron(np.eye(LANES // HEAD_DIM), np.full((HEAD_DIM, HEAD_DIM), 1.0 / HEAD_DIM)), BF16)

    xs = x.reshape(lat_rows, d)
    tail = ctx.reshape(ctx_rows, d)
    for i in range(depth):
        mods = mods_all[i]
        reads_ctx = i % 2 == 0
        later_reads_ctx = any(j % 2 == 0 for j in range(i + 1, depth))
        n1 = all_tiles if (reads_ctx or later_reads_ctx) else lat_tiles
        n2 = all_tiles if later_reads_ctx else lat_tiles
        common = dict(group_of_tile=group_of_tile)

        xs = _half_ffn(xs, mods, norm_g[i, 0][None], ffn_w13, ffn_w2, layer=i, half=0, n_tiles=n1,
                       tail=tail if i == 0 else None, **common)

        if reads_ctx:
            e = i // 2
            w_in = w_in_a[e]
            w_in = jnp.concatenate([w_in[:, :ATTN_WIDTH][:, head_cols], w_in[:, ATTN_WIDTH:]], axis=1).astype(BF16)
            w_out = jnp.concatenate([w_out_a[e][:ATTN_WIDTH][head_cols], w_out_a[e][ATTN_WIDTH:]], axis=0).astype(BF16)
            gq = jnp.tile(qk_norm_g[e, 0], LANES // HEAD_DIM)[None]
            gk = jnp.tile(qk_norm_g[e, 1], LANES // HEAD_DIM)[None]
            qt, k, vt, p = _project_a(xs, mods, norm_g[i, 1][None], w_in, gq, gk, head_mean,
                                      rope_by_row, rope_by_col, n_tiles=n1, lat_tiles=lat_tiles,
                                      tiles_per_batch=tiles_per_batch, **common)
            attn = _attention(qt, k, vt, batch=batch, seq=seq, ctx_len=ctx_len, latent=True)
            attn_c = (_attention(qt, k, vt, batch=batch, seq=seq, ctx_len=ctx_len, latent=False)
                      if later_reads_ctx else attn)
            xs = _combine_a(xs, mods, attn, attn_c, p, pool_w[e].astype(BF16), pool_scale[e][None], w_out,
                            n_tiles=n2, lat_rows=lat_rows, seq=seq, ctx_len=ctx_len, **common)
        else:
            o = i // 2
            xs = _mixer_c(xs, mods, norm_g[i, 1][None], w_in_c[o].astype(BF16), v_norm_g[o][None],
                          w_sp[o].astype(BF16), b_sp[o].T, w_out_c[o].astype(BF16), n_tiles=n2, **common)

        xs = _half_ffn(xs, mods, norm_g[i, 2][None], ffn_w13, ffn_w2, layer=i, half=1, n_tiles=n2, **common)
    return xs[:lat_rows].reshape(batch, seq, d)
```

```python
import functools

import jax
import jax.numpy as jnp
import numpy as np
from jax import lax
from jax.experimental import pallas as pl
from jax.experimental.pallas import tpu as pltpu

F32 = jnp.float32
BF16 = jnp.bfloat16

GRID_W = 64
HEAD_DIM = 64
N_Q_HEADS = 8
N_KV_HEADS = 2
Q_GROUP = N_Q_HEADS // N_KV_HEADS
ATTN_WIDTH = N_Q_HEADS * HEAD_DIM
KV_WIDTH = N_KV_HEADS * HEAD_DIM
ATTN_SCALE = HEAD_DIM ** -0.5
Q_SCALE_LOG2 = float(ATTN_SCALE * np.log2(np.e))
ROPE_THETA = 10000.0
POOL_WINDOWS = (2, 4, 8, 16)
POOL_GROUP_DIM = 128
CHUNK = 128
GMLP_GROUPS = 8
N_MOD = 9
EPS = 1e-6

LANES = 128
MXU_WIDTH = 256
SUBLANES = 8
VMEM_LIMIT_BYTES = 56 * 1024 * 1024

TM = 512
FF_CHUNK = 256
TQ = 128
TK = 512
HALO = 8


def _cparams(n_axes):
    return pltpu.CompilerParams(
        dimension_semantics=("arbitrary",) * n_axes,
        vmem_limit_bytes=VMEM_LIMIT_BYTES)


def _resident(shape):
    zeros = (0,) * len(shape)
    return pl.BlockSpec(shape, lambda *_: zeros, pipeline_mode=pl.Buffered(1))


def _modulate(x, g, shift, scale):
    ms = jnp.mean(x * x, axis=-1, keepdims=True)
    return x * lax.rsqrt(ms + EPS) * (g * (1.0 + scale)) + shift


def _mod_kernel(c_ref, w_ref, b_ref, o_ref):
    c = c_ref[...]
    s = (c * jax.nn.sigmoid(c)).astype(BF16)
    o_ref[0, 0] = jnp.dot(s, w_ref[0].astype(BF16), preferred_element_type=F32) + b_ref[0, 0]


def _modulation(c_rows, w_mod, b_mod):
    depth, d, _ = w_mod.shape
    rows = c_rows.shape[0]
    out = pl.pallas_call(
        _mod_kernel,
        grid=(depth, N_MOD),
        in_specs=[
            pl.BlockSpec((rows, d), lambda i, j: (0, 0)),
            pl.BlockSpec((1, d, d), lambda i, j: (i, 0, j)),
            pl.BlockSpec((1, 1, 1, d), lambda i, j: (i, j, 0, 0)),
        ],
        out_specs=pl.BlockSpec((1, 1, rows, d), lambda i, j: (i, j, 0, 0)),
        out_shape=jax.ShapeDtypeStruct((depth, N_MOD, rows, d), F32),
        compiler_params=_cparams(2),
        name="adaln_modulation",
    )(c_rows, w_mod, b_mod.reshape(depth, N_MOD, 1, d))
    return out


def _ffn_kernel(*refs, k0, head_tiles, layer, half):
    if head_tiles is None:
        x_ref, mod_ref, g_ref, w13_hbm, w2_hbm, o_ref, w13_ref, w2_ref, sem, h_ref, act_ref, acc_ref = refs
        x = x_ref[...]
    else:
        (x_ref, tail_ref, mod_ref, g_ref, w13_hbm, w2_hbm, o_ref, w13_ref, w2_ref, sem,
         h_ref, act_ref, acc_ref) = refs
        x = jnp.where(pl.program_id(0) < head_tiles, x_ref[...], tail_ref[...])
    mod = mod_ref[0]
    d_ff = w2_ref.shape[0]
    n_chunks = d_ff // FF_CHUNK
    first_step = pl.program_id(0) == 0

    def chunk(c, offset=0):
        start = c * FF_CHUNK + offset
        return pl.ds(start if isinstance(c, int) else pl.multiple_of(start, LANES), FF_CHUNK)

    def up_copy(c, part):
        cols = chunk(c, part * d_ff)
        return pltpu.make_async_copy(w13_hbm.at[layer, half, :, cols], w13_ref.at[:, cols], sem.at[part, c])

    def down_copy(c):
        return pltpu.make_async_copy(w2_hbm.at[layer, half, chunk(c), :], w2_ref.at[chunk(c), :], sem.at[2, c])

    def half_step(weights_in_flight):
        h_ref[...] = _modulate(x, g_ref[...], mod[k0:k0 + 1], mod[k0 + 1:k0 + 2]).astype(BF16)

        def gate(slot, c):
            if weights_in_flight:
                up_copy(c, 0).wait()
                up_copy(c, 1).wait()
            h = h_ref[...]
            a = jnp.dot(h, w13_ref[:, chunk(c)].astype(BF16), preferred_element_type=F32)
            b = jnp.dot(h, w13_ref[:, chunk(c, d_ff)].astype(BF16), preferred_element_type=F32)
            act_ref[slot] = (a * jax.nn.sigmoid(a) * b).astype(BF16)

        def down(slot, c, first=False):
            if weights_in_flight:
                down_copy(c).wait()
            part = jnp.dot(act_ref[slot], w2_ref[chunk(c), :].astype(BF16), preferred_element_type=F32)
            acc_ref[...] = part if first else acc_ref[...] + part

        assert n_chunks >= 2
        gate(0, 0)
        gate(1, 1)
        down(0, 0, first=True)

        def body(j, carry):
            c = 2 * j
            gate(0, c + 2)
            down(1, c + 1)
            gate(1, c + 3)
            down(0, c + 2)
            return carry

        lax.fori_loop(0, (n_chunks - 2) // 2, body, 0, unroll=True)
        if n_chunks % 2:
            gate(0, n_chunks - 1)
            down(1, n_chunks - 2)
            down(0, n_chunks - 1)
        else:
            down(1, n_chunks - 1)
        o_ref[...] = x + 0.5 * mod[k0 + 2:k0 + 3] * acc_ref[...]

    @pl.when(first_step)
    def _():
        for c in range(n_chunks):
            up_copy(c, 0).start()
            up_copy(c, 1).start()
            down_copy(c).start()
        half_step(weights_in_flight=True)

    @pl.when(jnp.logical_not(first_step))
    def _():
        half_step(weights_in_flight=False)


def _half_ffn(xs, mods, g, w13, w2, *, layer, half, n_tiles, group_of_tile, tail=None):
    d = xs.shape[1]
    d_ff = w2.shape[2]
    k0 = 6 * half
    in_hbm = pl.BlockSpec(memory_space=pl.ANY)
    if tail is None:
        head_tiles = None
        stream, stream_specs = [xs], [pl.BlockSpec((TM, d), lambda t: (t, 0))]
    else:
        head_tiles = xs.shape[0] // TM
        stream = [xs, tail]
        stream_specs = [pl.BlockSpec((TM, d), lambda t: (jnp.minimum(t, head_tiles - 1), 0)),
                        pl.BlockSpec((TM, d), lambda t: (jnp.maximum(t - head_tiles, 0), 0))]
    return pl.pallas_call(
        functools.partial(_ffn_kernel, k0=k0, head_tiles=head_tiles, layer=layer, half=half),
        grid=(n_tiles,),
        in_specs=stream_specs + [
            pl.BlockSpec((1, N_MOD, d), lambda t: (group_of_tile(t), 0, 0)),
            pl.BlockSpec((1, d), lambda t: (0, 0)),
            in_hbm,
            in_hbm,
        ],
        out_specs=pl.BlockSpec((TM, d), lambda t: (t, 0)),
        out_shape=jax.ShapeDtypeStruct((n_tiles * TM, d), F32),
        scratch_shapes=[pltpu.VMEM((d, 2 * d_ff), F32), pltpu.VMEM((d_ff, d), F32),
                        pltpu.SemaphoreType.DMA((3, d_ff // FF_CHUNK)),
                        pltpu.VMEM((TM, d), BF16), pltpu.VMEM((2, TM, FF_CHUNK), BF16),
                        pltpu.VMEM((TM, d), F32)],
        compiler_params=_cparams(1),
        name="half_ffn",
    )(*stream, mods, g, w13, w2)


def _rope(x, cos, sin_hi, sin_lo):
    return (x * cos + pltpu.roll(x, 16, axis=1) * sin_hi
            + pltpu.roll(x, LANES - 16, axis=1) * sin_lo)


def _proj_a_kernel(x_ref, mod_ref, g_ref, w_ref, gq_ref, gk_ref, hsum_ref,
                   rope_row_ref, rope_col_ref, qt_ref, k_ref, vt_ref, p_ref, *, lat_tiles):
    mod = mod_ref[0]
    h = _modulate(x_ref[...], g_ref[...], mod[3:4], mod[4:5]).astype(BF16)
    z = jnp.dot(h, w_ref[...], preferred_element_type=F32)

    by_row, by_col = rope_row_ref[...], rope_col_ref[...]
    rope = jnp.concatenate([jnp.broadcast_to(by_row[r:r + 1, :], by_col.shape) + by_col
                            for r in range(TM // GRID_W)], axis=0)
    is_ctx = pl.program_id(0) >= lat_tiles
    cos = jnp.where(is_ctx, 1.0, rope[:, :LANES])
    shi = jnp.where(is_ctx, 0.0, rope[:, LANES:2 * LANES])
    slo = jnp.where(is_ctx, 0.0, rope[:, 2 * LANES:])
    hsum = hsum_ref[...]

    def norm_rope(zb, gain):
        ms = jnp.dot((zb * zb).astype(BF16), hsum, preferred_element_type=F32)
        return _rope(zb * lax.rsqrt(ms + EPS) * gain, cos, shi, slo)

    for j in range(ATTN_WIDTH // LANES):
        qb = norm_rope(z[:, j * LANES:(j + 1) * LANES], gq_ref[...])
        qt_ref[j * LANES:(j + 1) * LANES, :] = (qb * Q_SCALE_LOG2).T.astype(BF16)
    k_ref[...] = norm_rope(z[:, ATTN_WIDTH:ATTN_WIDTH + KV_WIDTH], gk_ref[...]).astype(BF16)
    vt_ref[0] = z[:, ATTN_WIDTH + KV_WIDTH:ATTN_WIDTH + 2 * KV_WIDTH].T.astype(BF16)
    p_ref[...] = z[:, ATTN_WIDTH + 2 * KV_WIDTH:]


def _project_a(xs, mods, g, w_in, gq, gk, hsum, rope_by_row, rope_by_col, *, n_tiles, group_of_tile,
               lat_tiles, tiles_per_batch):
    d = xs.shape[1]
    nt = n_tiles * TM
    grid_rows = TM // GRID_W
    pool_width = w_in.shape[1] - ATTN_WIDTH - 2 * KV_WIDTH
    row_tile = lambda w: pl.BlockSpec((TM, w), lambda t: (t, 0))
    const2 = lambda shape: pl.BlockSpec(shape, lambda t: (0, 0))
    return pl.pallas_call(
        functools.partial(_proj_a_kernel, lat_tiles=lat_tiles),
        grid=(n_tiles,),
        in_specs=[
            row_tile(d),
            pl.BlockSpec((1, N_MOD, d), lambda t: (group_of_tile(t), 0, 0)),
            const2((1, d)),
            _resident(w_in.shape),
            const2((1, LANES)), const2((1, LANES)), const2((LANES, LANES)),
            pl.BlockSpec((grid_rows, 3 * LANES), lambda t: (t % tiles_per_batch, 0)),
            const2(rope_by_col.shape),
        ],
        out_specs=[
            pl.BlockSpec((ATTN_WIDTH, TM), lambda t: (0, t)),
            row_tile(KV_WIDTH),
            pl.BlockSpec((1, KV_WIDTH, TM), lambda t: (t, 0, 0)),
            row_tile(pool_width),
        ],
        out_shape=[
            jax.ShapeDtypeStruct((ATTN_WIDTH, nt), BF16),
            jax.ShapeDtypeStruct((nt, KV_WIDTH), BF16),
            jax.ShapeDtypeStruct((n_tiles, KV_WIDTH, TM), BF16),
            jax.ShapeDtypeStruct((nt, pool_width), F32),
        ],
        compiler_params=_cparams(1),
        name="mixer_a_project",
    )(xs, mods, g, w_in, gq, gk, hsum, rope_by_row, rope_by_col)


def _attn_kernel(*refs, n_lat_steps):
    if n_lat_steps:
        qt_ref, kc_ref, vtc_ref, kl_ref, vtl_ref, o_ref, m_ref, l_ref, acc_ref, s_ref = refs
    else:
        qt_ref, kc_ref, vtc_ref, o_ref, m_ref, l_ref, acc_ref = refs
    tq = qt_ref.shape[1]
    n_blocks = ATTN_WIDTH // LANES
    top = lax.broadcasted_iota(jnp.int32, (LANES, tq), 0) < HEAD_DIM
    pieces = []
    for j in range(n_blocks):
        blk = qt_ref[j * LANES:(j + 1) * LANES, :]
        zero = jnp.zeros_like(blk)
        pieces += [jnp.where(top, blk, zero), jnp.where(top, zero, blk)]
    qqt = jnp.concatenate(pieces, axis=1)

    m_ref[...] = jnp.full_like(m_ref, -jnp.inf)
    l_ref[...] = jnp.zeros_like(l_ref)
    acc_ref[...] = jnp.zeros_like(acc_ref)

    col_tiles = [slice(ct * MXU_WIDTH, (ct + 1) * MXU_WIDTH) for ct in range(qqt.shape[1] // MXU_WIDTH)]

    def accumulate(s, vt, cs):
        m_old = m_ref[:, cs]
        m_new = jnp.maximum(m_old, jnp.max(s, axis=0, keepdims=True))
        alpha = jnp.exp2(m_old - m_new)
        p = jnp.exp2(s - m_new)
        l_ref[:, cs] = alpha * l_ref[:, cs] + jnp.sum(p, axis=0, keepdims=True)
        acc_ref[:, cs] = alpha * acc_ref[:, cs] + jnp.dot(vt, p.astype(BF16), preferred_element_type=F32)
        m_ref[:, cs] = m_new

    def lat_keys(i):
        return kl_ref[pl.ds(pl.multiple_of(i * TK, TK), TK), :]

    def step(nxt, cur):
        for cs in col_tiles:
            if nxt is not None:
                s_ref[nxt[0], :, cs] = jnp.dot(nxt[1], qqt[:, cs], preferred_element_type=F32)
            if cur is not None:
                accumulate(cur[0][:, cs], cur[1], cs)

    s_ctx = jnp.dot(kc_ref[...], qqt, preferred_element_type=F32)
    if n_lat_steps:
        step((1, lat_keys(0)), (s_ctx, vtc_ref[0]))

        def body(j, carry):
            i = 2 * j
            step((0, lat_keys(i + 1)), (s_ref.at[1], vtl_ref[i]))
            step((1, lat_keys(i + 2)), (s_ref.at[0], vtl_ref[i + 1]))
            return carry
        lax.fori_loop(0, n_lat_steps // 2 - 1, body, 0, unroll=3)
        step((0, lat_keys(n_lat_steps - 1)), (s_ref.at[1], vtl_ref[n_lat_steps - 2]))
        step(None, (s_ref.at[0], vtl_ref[n_lat_steps - 1]))
    else:
        step(None, (s_ctx, vtc_ref[0]))

    out_t = acc_ref[...] / l_ref[...]
    for j in range(n_blocks):
        o_top = out_t[:, (2 * j) * tq:(2 * j + 1) * tq]
        o_bottom = out_t[:, (2 * j + 1) * tq:(2 * j + 2) * tq]
        o_ref[:, j * LANES:(j + 1) * LANES] = jnp.where(top, o_top, o_bottom).T.astype(o_ref.dtype)


def _attention(qt, k, vt, *, batch, seq, ctx_len, latent):
    lat_rows = batch * seq
    n_q = (seq if latent else ctx_len) // TQ
    q_blk0 = 0 if latent else lat_rows // TQ
    ctx_per_slab = TM // ctx_len
    ctx_slab0 = lat_rows // TM
    in_specs = [
        pl.BlockSpec((ATTN_WIDTH, TQ), lambda b, i: (0, q_blk0 + b * n_q + i)),
        pl.BlockSpec((ctx_len, KV_WIDTH), lambda b, i: (lat_rows // ctx_len + b, 0)),
        pl.BlockSpec((1, KV_WIDTH, ctx_len),
                     lambda b, i: (ctx_slab0 + b // ctx_per_slab, 0, b % ctx_per_slab)),
    ]
    operands = [qt, k, vt]
    if latent:
        in_specs += [pl.BlockSpec((seq, KV_WIDTH), lambda b, i: (b, 0)),
                     pl.BlockSpec((seq // TM, KV_WIDTH, TM), lambda b, i: (b, 0, 0))]
        operands += [k, vt]
    cols = N_Q_HEADS * TQ
    scratch = [pltpu.VMEM((1, cols), F32), pltpu.VMEM((1, cols), F32), pltpu.VMEM((KV_WIDTH, cols), F32)]
    if latent:
        assert (seq // TK) % 2 == 0 and seq // TK >= 2
        scratch.append(pltpu.VMEM((2, TK, cols), F32))
    return pl.pallas_call(
        functools.partial(_attn_kernel, n_lat_steps=seq // TK if latent else 0),
        grid=(batch, n_q),
        in_specs=in_specs,
        out_specs=pl.BlockSpec((TQ, ATTN_WIDTH), lambda b, i: (b * n_q + i, 0)),
        out_shape=jax.ShapeDtypeStruct((batch * n_q * TQ, ATTN_WIDTH), BF16),
        scratch_shapes=scratch,
        compiler_params=_cparams(2),
        name="gqa_attention_latent" if latent else "gqa_attention_ctx",
    )(*operands)


def _shift_rows(a, d):
    return pltpu.roll(a, d % a.shape[0], axis=0)


def _pool_group(pe, w, edge):
    def before(a, span):
        shifted = _shift_rows(a, span)
        return shifted if edge is None else jnp.where(edge[0] >= span, shifted, jnp.zeros_like(a))

    def after(a, span):
        shifted = _shift_rows(a, -span)
        return shifted if edge is None else jnp.where(edge[0] + span < edge[1], shifted, jnp.zeros_like(a))

    left = before(pe, 1)
    right = pe
    span = 1
    while 2 * span <= w // 2:
        left = left + before(left, span)
        right = right + after(right, span)
        span *= 2
    if edge is None:
        return (left + right) * (1.0 / w)
    pos, n = edge
    count = jnp.maximum(jnp.minimum(pos + w // 2, n) - jnp.maximum(pos - w // 2, 0), 1)
    return (left + right) / count.astype(F32)


def _rem_static(r, m):
    return r & (m - 1) if m & (m - 1) == 0 else lax.rem(r, m)


def _comb_a_kernel(x_ref, mod_ref, attn_ref, attn_tail_ref, p_ref, pprev_ref, pnext_ref, pw_ref, ps_ref,
                   wo_ref, o_ref, pooled_ref, *, lat_rows, seq, ctx_len, attn_head_tiles):
    tile = pl.program_id(0)
    tiles_per_seq = seq // TM

    def mix(edge):
        attn = jnp.where(tile < attn_head_tiles, attn_ref[...], attn_tail_ref[...])
        mixed = jnp.dot(attn, wo_ref[:ATTN_WIDTH, :], preferred_element_type=F32)
        for gi, w in enumerate(POOL_WINDOWS):
            cols = slice(gi * POOL_GROUP_DIM, (gi + 1) * POOL_GROUP_DIM)
            pe = jnp.concatenate([pprev_ref[:, cols], p_ref[:, cols], pnext_ref[:, cols]], axis=0)
            centred = (_pool_group(pe, w, edge) - pe)[HALO:HALO + TM]
            pooled = jnp.dot(centred.astype(BF16), pw_ref[gi], preferred_element_type=F32) * ps_ref[:, cols]
            pooled_ref[:, cols] = pooled.astype(BF16)
        mixed += jnp.dot(pooled_ref[...], wo_ref[ATTN_WIDTH:, :], preferred_element_type=F32)
        o_ref[...] = x_ref[...] + mod_ref[0][5:6] * mixed

    seq_tile = lax.rem(tile, tiles_per_seq)
    at_edge = (tile >= lat_rows // TM) | (seq_tile == 0) | (seq_tile == tiles_per_seq - 1)

    @pl.when(at_edge)
    def _():
        rows = TM + 2 * HALO
        r = tile * TM - HALO + lax.broadcasted_iota(jnp.int32, (rows, POOL_GROUP_DIM), 0)
        is_lat = r < lat_rows
        n = jnp.where(is_lat, seq, ctx_len)
        pos = jnp.where(is_lat, _rem_static(r, seq), _rem_static(r - lat_rows, ctx_len))
        mix((pos, n))

    @pl.when(jnp.logical_not(at_edge))
    def _():
        mix(None)


def _combine_a(xs, mods, attn, attn_tail, p, pool_w, pool_scale, w_out, *, n_tiles, group_of_tile,
               lat_rows, seq, ctx_len):
    d = xs.shape[1]
    pool_width = p.shape[1]
    halo_per_tile = TM // HALO
    last_halo_blk = p.shape[0] // HALO - 1
    head_tiles = attn.shape[0] // TM
    return pl.pallas_call(
        functools.partial(_comb_a_kernel, lat_rows=lat_rows, seq=seq, ctx_len=ctx_len,
                          attn_head_tiles=head_tiles),
        grid=(n_tiles,),
        in_specs=[
            pl.BlockSpec((TM, d), lambda t: (t, 0)),
            pl.BlockSpec((1, N_MOD, d), lambda t: (group_of_tile(t), 0, 0)),
            pl.BlockSpec((TM, ATTN_WIDTH), lambda t: (jnp.minimum(t, head_tiles - 1), 0)),
            pl.BlockSpec((TM, ATTN_WIDTH), lambda t: (jnp.maximum(t - head_tiles, 0), 0)),
            pl.BlockSpec((TM, pool_width), lambda t: (t, 0)),
            pl.BlockSpec((HALO, pool_width), lambda t: (jnp.maximum(t * halo_per_tile - 1, 0), 0)),
            pl.BlockSpec((HALO, pool_width),
                         lambda t: (jnp.minimum((t + 1) * halo_per_tile, last_halo_blk), 0)),
            _resident(pool_w.shape),
            pl.BlockSpec((1, pool_width), lambda t: (0, 0)),
            _resident(w_out.shape),
        ],
        out_specs=pl.BlockSpec((TM, d), lambda t: (t, 0)),
        out_shape=jax.ShapeDtypeStruct((n_tiles * TM, d), F32),
        scratch_shapes=[pltpu.VMEM((TM, pool_width), BF16)],
        compiler_params=_cparams(1),
        name="mixer_a_combine",
    )(xs, mods, attn, attn_tail, p, p, p, pool_w, pool_scale, w_out)


def _gelu_tanh(x):
    return 0.5 * x * (1.0 + jnp.tanh(np.sqrt(2.0 / np.pi).astype(np.float32) * (x + 0.044715 * (x * x * x))))


def _mixer_c_kernel(x_ref, mod_ref, g_ref, wi_ref, vg_ref, wsp_ref, bsp_ref, wo_ref, o_ref, gated_ref):
    x = x_ref[...]
    mod = mod_ref[0]
    width = wo_ref.shape[0]
    h = _modulate(x, g_ref[...], mod[3:4], mod[4:5]).astype(BF16)
    z = _gelu_tanh(jnp.dot(h, wi_ref[...], preferred_element_type=F32))
    u = z[:, :width]
    v = z[:, width:]
    v = (v * lax.rsqrt(jnp.mean(v * v, axis=-1, keepdims=True) + EPS) * vg_ref[...]).astype(BF16)
    bias = bsp_ref[...]
    for g in range(GMLP_GROUPS):
        cols = slice(g * CHUNK, (g + 1) * CHUNK)
        b_col = bias[:, g:g + 1]
        for c in range(TM // CHUNK):
            rows = slice(c * CHUNK, (c + 1) * CHUNK)
            sv = jnp.dot(wsp_ref[g], v[rows, cols], preferred_element_type=F32) + b_col
            gated_ref[rows, cols] = (u[rows, cols] * sv).astype(BF16)
    o_ref[...] = x + mod[5:6] * jnp.dot(gated_ref[...], wo_ref[...], preferred_element_type=F32)


def _mixer_c(xs, mods, g, w_in, v_g, w_sp, b_sp_cols, w_out, *, n_tiles, group_of_tile):
    d = xs.shape[1]
    width = w_out.shape[0]
    return pl.pallas_call(
        _mixer_c_kernel,
        grid=(n_tiles,),
        in_specs=[
            pl.BlockSpec((TM, d), lambda t: (t, 0)),
            pl.BlockSpec((1, N_MOD, d), lambda t: (group_of_tile(t), 0, 0)),
            pl.BlockSpec((1, d), lambda t: (0, 0)),
            _resident(w_in.shape),
            pl.BlockSpec((1, width), lambda t: (0, 0)),
            _resident(w_sp.shape),
            pl.BlockSpec(b_sp_cols.shape, lambda t: (0, 0)),
            _resident(w_out.shape),
        ],
        out_specs=pl.BlockSpec((TM, d), lambda t: (t, 0)),
        out_shape=jax.ShapeDtypeStruct((n_tiles * TM, d), F32),
        scratch_shapes=[pltpu.VMEM((TM, width), BF16)],
        compiler_params=_cparams(1),
        name="mixer_c_gmlp",
    )(xs, mods, g, w_in, v_g, w_sp, b_sp_cols, w_out)


def _rope_tables(seq):
    half = HEAD_DIM // 2
    inv_freq = ROPE_THETA ** (-jnp.arange(0, half, 2, dtype=F32) / half)

    def parts(n, row_axis):
        ang = jnp.arange(n).astype(F32)[:, None] * inv_freq[None, :]
        cos, sin, zero = jnp.cos(ang), jnp.sin(ang), jnp.zeros_like(ang)
        if row_axis:
            heads = ([cos, cos, zero, zero], [zero, sin, zero, zero], [-sin, zero, zero, zero])
        else:
            heads = ([zero, zero, cos, cos], [zero, zero, zero, sin], [zero, zero, -sin, zero])
        return jnp.concatenate([blk for head in heads for blk in head * (LANES // HEAD_DIM)], axis=1)

    return parts(seq // GRID_W, True), parts(GRID_W, False)


def _head_block_order():
    order = []
    for j in range(Q_GROUP):
        order += [j, j + Q_GROUP]
    return np.concatenate([np.arange(h * HEAD_DIM, (h + 1) * HEAD_DIM) for h in order])


def kernel(x, c, ctx, c_ctx, w_mod, b_mod, norm_g, ffn_w13, ffn_w2, w_in_a, qk_norm_g, pool_w,
           pool_scale, w_out_a, w_in_c, v_norm_g, w_sp, b_sp, w_out_c):
    batch, seq, d = x.shape
    ctx_len = ctx.shape[1]
    depth = w_mod.shape[0]
    d_ff = ffn_w2.shape[2]
    lat_rows = batch * seq
    ctx_rows = batch * ctx_len
    assert seq % TM == 0 and ctx_rows % TM == 0 and seq % TK == 0 and ctx_len % TQ == 0
    assert ctx_len % HALO == 0 and lat_rows % ctx_len == 0 and d_ff % FF_CHUNK == 0
    assert TK == TM and TM % ctx_len == 0
    assert seq % GRID_W == 0 and TM % (GRID_W * SUBLANES) == 0
    lat_tiles = lat_rows // TM
    all_tiles = lat_tiles + ctx_rows // TM
    tiles_per_batch = seq // TM
    group_of_tile = lambda t: jnp.minimum(t // tiles_per_batch, batch)

    c_rows = jnp.concatenate([c, c_ctx[None, :], jnp.zeros((SUBLANES - batch - 1, d), F32)], axis=0)
    mods_all = _modulation(c_rows, w_mod, b_mod)
    mods_all = jnp.transpose(mods_all[:, :, :batch + 1], (0, 2, 1, 3))

    head_cols = _head_block_order()
    rope_by_row, rope_by_col = _rope_tables(seq)
    head_mean = jnp.asarray(np.kron(np.eye(LANES // HEAD_DIM), np.full((HEAD_DIM, HEAD_DIM), 1.0 / HEAD_DIM)), BF16)

    xs = x.reshape(lat_rows, d)
    tail = ctx.reshape(ctx_rows, d)
    for i in range(depth):
        mods = mods_all[i]
        reads_ctx = i % 2 == 0
        later_reads_ctx = any(j % 2 == 0 for j in range(i + 1, depth))
        n1 = all_tiles if (reads_ctx or later_reads_ctx) else lat_tiles
        n2 = all_tiles if later_reads_ctx else lat_tiles
        common = dict(group_of_tile=group_of_tile)

        xs = _half_ffn(xs, mods, norm_g[i, 0][None], ffn_w13, ffn_w2, layer=i, half=0, n_tiles=n1,
                       tail=tail if i == 0 else None, **common)

        if reads_ctx:
            e = i // 2
            w_in = w_in_a[e]
            w_in = jnp.concatenate([w_in[:, :ATTN_WIDTH][:, head_cols], w_in[:, ATTN_WIDTH:]], axis=1).astype(BF16)
            w_out = jnp.concatenate([w_out_a[e][:ATTN_WIDTH][head_cols], w_out_a[e][ATTN_WIDTH:]], axis=0).astype(BF16)
            gq = jnp.tile(qk_norm_g[e, 0], LANES // HEAD_DIM)[None]
            gk = jnp.tile(qk_norm_g[e, 1], LANES // HEAD_DIM)[None]
            qt, k, vt, p = _project_a(xs, mods, norm_g[i, 1][None], w_in, gq, gk, head_mean,
                                      rope_by_row, rope_by_col, n_tiles=n1, lat_tiles=lat_tiles,
                                      tiles_per_batch=tiles_per_batch, **common)
            attn = _attention(qt, k, vt, batch=batch, seq=seq, ctx_len=ctx_len, latent=True)
            attn_c = (_attention(qt, k, vt, batch=batch, seq=seq, ctx_len=ctx_len, latent=False)
                      if later_reads_ctx else attn)
            xs = _combine_a(xs, mods, attn, attn_c, p, pool_w[e].astype(BF16), pool_scale[e][None], w_out,
                            n_tiles=n2, lat_rows=lat_rows, seq=seq, ctx_len=ctx_len, **common)
        else:
            o = i // 2
            xs = _mixer_c(xs, mods, norm_g[i, 1][None], w_in_c[o].astype(BF16), v_norm_g[o][None],
                          w_sp[o].astype(BF16), b_sp[o].T, w_out_c[o].astype(BF16), n_tiles=n2, **common)

        xs = _half_ffn(xs, mods, norm_g[i, 2][None], ffn_w13, ffn_w2, layer=i, half=1, n_tiles=n2, **common)
    return xs[:lat_rows].reshape(batch, seq, d)
```

```python
import functools

import jax
import jax.numpy as jnp
import numpy as np
from jax import lax
from jax.experimental import pallas as pl
from jax.experimental.pallas import tpu as pltpu

F32 = jnp.float32
BF16 = jnp.bfloat16

GRID_W = 64
HEAD_DIM = 64
N_Q_HEADS = 8
N_KV_HEADS = 2
Q_GROUP = N_Q_HEADS // N_KV_HEADS
ATTN_WIDTH = N_Q_HEADS * HEAD_DIM
KV_WIDTH = N_KV_HEADS * HEAD_DIM
ATTN_SCALE = HEAD_DIM ** -0.5
Q_SCALE_LOG2 = float(ATTN_SCALE * np.log2(np.e))
ROPE_THETA = 10000.0
POOL_WINDOWS = (2, 4, 8, 16)
POOL_GROUP_DIM = 128
CHUNK = 128
GMLP_GROUPS = 8
N_MOD = 9
EPS = 1e-6

LANES = 128
MXU_WIDTH = 256
SUBLANES = 8
VMEM_LIMIT_BYTES = 56 * 1024 * 1024

TM = 512
FF_CHUNK = 256
TQ = 256
TK = 512
HALO = 8


def _cparams(n_axes):
    return pltpu.CompilerParams(
        dimension_semantics=("arbitrary",) * n_axes,
        vmem_limit_bytes=VMEM_LIMIT_BYTES)


def _resident(shape):
    zeros = (0,) * len(shape)
    return pl.BlockSpec(shape, lambda *_: zeros, pipeline_mode=pl.Buffered(1))


def _modulate(x, g, shift, scale):
    ms = jnp.mean(x * x, axis=-1, keepdims=True)
    return x * lax.rsqrt(ms + EPS) * (g * (1.0 + scale)) + shift


def _mod_kernel(c_ref, w_ref, b_ref, o_ref):
    c = c_ref[...]
    s = (c * jax.nn.sigmoid(c)).astype(BF16)
    o_ref[0, 0] = jnp.dot(s, w_ref[0].astype(BF16), preferred_element_type=F32) + b_ref[0, 0]


def _modulation(c_rows, w_mod, b_mod):
    depth, d, _ = w_mod.shape
    rows = c_rows.shape[0]
    out = pl.pallas_call(
        _mod_kernel,
        grid=(depth, N_MOD),
        in_specs=[
            pl.BlockSpec((rows, d), lambda i, j: (0, 0)),
            pl.BlockSpec((1, d, d), lambda i, j: (i, 0, j)),
            pl.BlockSpec((1, 1, 1, d), lambda i, j: (i, j, 0, 0)),
        ],
        out_specs=pl.BlockSpec((1, 1, rows, d), lambda i, j: (i, j, 0, 0)),
        out_shape=jax.ShapeDtypeStruct((depth, N_MOD, rows, d), F32),
        compiler_params=_cparams(2),
        name="adaln_modulation",
    )(c_rows, w_mod, b_mod.reshape(depth, N_MOD, 1, d))
    return out


def _ffn_kernel(*refs, k0, head_tiles, layer, half):
    if head_tiles is None:
        x_ref, mod_ref, g_ref, w13_hbm, w2_hbm, o_ref, w13_ref, w2_ref, sem, h_ref, act_ref, acc_ref = refs
        x = x_ref[...]
    else:
        (x_ref, tail_ref, mod_ref, g_ref, w13_hbm, w2_hbm, o_ref, w13_ref, w2_ref, sem,
         h_ref, act_ref, acc_ref) = refs
        x = jnp.where(pl.program_id(0) < head_tiles, x_ref[...], tail_ref[...])
    mod = mod_ref[0]
    d_ff = w2_ref.shape[0]
    n_chunks = d_ff // FF_CHUNK
    first_step = pl.program_id(0) == 0

    def chunk(c, offset=0):
        start = c * FF_CHUNK + offset
        return pl.ds(start if isinstance(c, int) else pl.multiple_of(start, LANES), FF_CHUNK)

    def up_copy(c, part):
        cols = chunk(c, part * d_ff)
        return pltpu.make_async_copy(w13_hbm.at[layer, half, :, cols], w13_ref.at[:, cols], sem.at[part, c])

    def down_copy(c):
        return pltpu.make_async_copy(w2_hbm.at[layer, half, chunk(c), :], w2_ref.at[chunk(c), :], sem.at[2, c])

    def half_step(weights_in_flight):
        h_ref[...] = _modulate(x, g_ref[...], mod[k0:k0 + 1], mod[k0 + 1:k0 + 2]).astype(BF16)

        def gate(slot, c):
            if weights_in_flight:
                up_copy(c, 0).wait()
                up_copy(c, 1).wait()
            h = h_ref[...]
            a = jnp.dot(h, w13_ref[:, chunk(c)].astype(BF16), preferred_element_type=F32)
            b = jnp.dot(h, w13_ref[:, chunk(c, d_ff)].astype(BF16), preferred_element_type=F32)
            act_ref[slot] = (a * jax.nn.sigmoid(a) * b).astype(BF16)

        def down(slot, c, first=False):
            if weights_in_flight:
                down_copy(c).wait()
            part = jnp.dot(act_ref[slot], w2_ref[chunk(c), :].astype(BF16), preferred_element_type=F32)
            acc_ref[...] = part if first else acc_ref[...] + part

        assert n_chunks >= 2
        gate(0, 0)
        gate(1, 1)
        down(0, 0, first=True)

        def body(j, carry):
            c = 2 * j
            gate(0, c + 2)
            down(1, c + 1)
            gate(1, c + 3)
            down(0, c + 2)
            return carry

        lax.fori_loop(0, (n_chunks - 2) // 2, body, 0, unroll=True)
        if n_chunks % 2:
            gate(0, n_chunks - 1)
            down(1, n_chunks - 2)
            down(0, n_chunks - 1)
        else:
            down(1, n_chunks - 1)
        o_ref[...] = x + 0.5 * mod[k0 + 2:k0 + 3] * acc_ref[...]

    @pl.when(first_step)
    def _():
        for c in range(n_chunks):
            up_copy(c, 0).start()
            up_copy(c, 1).start()
            down_copy(c).start()
        half_step(weights_in_flight=True)

    @pl.when(jnp.logical_not(first_step))
    def _():
        half_step(weights_in_flight=False)


def _half_ffn(xs, mods, g, w13, w2, *, layer, half, n_tiles, group_of_tile, tail=None):
    d = xs.shape[1]
    d_ff = w2.shape[2]
    k0 = 6 * half
    in_hbm = pl.BlockSpec(memory_space=pl.ANY)
    if tail is None:
        head_tiles = None
        stream, stream_specs = [xs], [pl.BlockSpec((TM, d), lambda t: (t, 0))]
    else:
        head_tiles = xs.shape[0] // TM
        stream = [xs, tail]
        stream_specs = [pl.BlockSpec((TM, d), lambda t: (jnp.minimum(t, head_tiles - 1), 0)),
                        pl.BlockSpec((TM, d), lambda t: (jnp.maximum(t - head_tiles, 0), 0))]
    return pl.pallas_call(
        functools.partial(_ffn_kernel, k0=k0, head_tiles=head_tiles, layer=layer, half=half),
        grid=(n_tiles,),
        in_specs=stream_specs + [
            pl.BlockSpec((1, N_MOD, d), lambda t: (group_of_tile(t), 0, 0)),
            pl.BlockSpec((1, d), lambda t: (0, 0)),
            in_hbm,
            in_hbm,
        ],
        out_specs=pl.BlockSpec((TM, d), lambda t: (t, 0)),
        out_shape=jax.ShapeDtypeStruct((n_tiles * TM, d), F32),
        scratch_shapes=[pltpu.VMEM((d, 2 * d_ff), F32), pltpu.VMEM((d_ff, d), F32),
                        pltpu.SemaphoreType.DMA((3, d_ff // FF_CHUNK)),
                        pltpu.VMEM((TM, d), BF16), pltpu.VMEM((2, TM, FF_CHUNK), BF16),
                        pltpu.VMEM((TM, d), F32)],
        compiler_params=_cparams(1),
        name="half_ffn",
    )(*stream, mods, g, w13, w2)


def _rope(x, cos, sin_hi, sin_lo):
    return (x * cos + pltpu.roll(x, 16, axis=1) * sin_hi
            + pltpu.roll(x, LANES - 16, axis=1) * sin_lo)


def _proj_a_kernel(x_ref, mod_ref, g_ref, w_ref, gq_ref, gk_ref, hsum_ref,
                   rope_row_ref, rope_col_ref, qt_ref, k_ref, vt_ref, p_ref, *, lat_tiles):
    mod = mod_ref[0]
    h = _modulate(x_ref[...], g_ref[...], mod[3:4], mod[4:5]).astype(BF16)
    z = jnp.dot(h, w_ref[...], preferred_element_type=F32)

    by_row, by_col = rope_row_ref[...], rope_col_ref[...]
    rope = jnp.concatenate([jnp.broadcast_to(by_row[r:r + 1, :], by_col.shape) + by_col
                            for r in range(TM // GRID_W)], axis=0)
    is_ctx = pl.program_id(0) >= lat_tiles
    cos = jnp.where(is_ctx, 1.0, rope[:, :LANES])
    shi = jnp.where(is_ctx, 0.0, rope[:, LANES:2 * LANES])
    slo = jnp.where(is_ctx, 0.0, rope[:, 2 * LANES:])
    hsum = hsum_ref[...]

    def norm_rope(zb, gain):
        ms = jnp.dot((zb * zb).astype(BF16), hsum, preferred_element_type=F32)
        return _rope(zb * lax.rsqrt(ms + EPS) * gain, cos, shi, slo)

    for j in range(ATTN_WIDTH // LANES):
        qb = norm_rope(z[:, j * LANES:(j + 1) * LANES], gq_ref[...])
        qt_ref[j * LANES:(j + 1) * LANES, :] = (qb * Q_SCALE_LOG2).T.astype(BF16)
    k_ref[...] = norm_rope(z[:, ATTN_WIDTH:ATTN_WIDTH + KV_WIDTH], gk_ref[...]).astype(BF16)
    vt_ref[0] = z[:, ATTN_WIDTH + KV_WIDTH:ATTN_WIDTH + 2 * KV_WIDTH].T.astype(BF16)
    p_ref[...] = z[:, ATTN_WIDTH + 2 * KV_WIDTH:]


def _project_a(xs, mods, g, w_in, gq, gk, hsum, rope_by_row, rope_by_col, *, n_tiles, group_of_tile,
               lat_tiles, tiles_per_batch):
    d = xs.shape[1]
    nt = n_tiles * TM
    grid_rows = TM // GRID_W
    pool_width = w_in.shape[1] - ATTN_WIDTH - 2 * KV_WIDTH
    row_tile = lambda w: pl.BlockSpec((TM, w), lambda t: (t, 0))
    const2 = lambda shape: pl.BlockSpec(shape, lambda t: (0, 0))
    return pl.pallas_call(
        functools.partial(_proj_a_kernel, lat_tiles=lat_tiles),
        grid=(n_tiles,),
        in_specs=[
            row_tile(d),
            pl.BlockSpec((1, N_MOD, d), lambda t: (group_of_tile(t), 0, 0)),
            const2((1, d)),
            _resident(w_in.shape),
            const2((1, LANES)), const2((1, LANES)), const2((LANES, LANES)),
            pl.BlockSpec((grid_rows, 3 * LANES), lambda t: (t % tiles_per_batch, 0)),
            const2(rope_by_col.shape),
        ],
        out_specs=[
            pl.BlockSpec((ATTN_WIDTH, TM), lambda t: (0, t)),
            row_tile(KV_WIDTH),
            pl.BlockSpec((1, KV_WIDTH, TM), lambda t: (t, 0, 0)),
            row_tile(pool_width),
        ],
        out_shape=[
            jax.ShapeDtypeStruct((ATTN_WIDTH, nt), BF16),
            jax.ShapeDtypeStruct((nt, KV_WIDTH), BF16),
            jax.ShapeDtypeStruct((n_tiles, KV_WIDTH, TM), BF16),
            jax.ShapeDtypeStruct((nt, pool_width), F32),
        ],
        compiler_params=_cparams(1),
        name="mixer_a_project",
    )(xs, mods, g, w_in, gq, gk, hsum, rope_by_row, rope_by_col)


def _attn_kernel(*refs, n_lat_steps):
    if n_lat_steps:
        qt_ref, kc_ref, vtc_ref, kl_ref, vtl_ref, o_ref, m_ref, l_ref, acc_ref, s_ref = refs
    else:
        qt_ref, kc_ref, vtc_ref, o_ref, m_ref, l_ref, acc_ref = refs
    tq = qt_ref.shape[1]
    n_blocks = ATTN_WIDTH // LANES
    top = lax.broadcasted_iota(jnp.int32, (LANES, tq), 0) < HEAD_DIM
    pieces = []
    for j in range(n_blocks):
        blk = qt_ref[j * LANES:(j + 1) * LANES, :]
        zero = jnp.zeros_like(blk)
        pieces += [jnp.where(top, blk, zero), jnp.where(top, zero, blk)]
    qqt = jnp.concatenate(pieces, axis=1)

    m_ref[...] = jnp.full_like(m_ref, -jnp.inf)
    l_ref[...] = jnp.zeros_like(l_ref)
    acc_ref[...] = jnp.zeros_like(acc_ref)

    col_tiles = [slice(ct * MXU_WIDTH, (ct + 1) * MXU_WIDTH) for ct in range(qqt.shape[1] // MXU_WIDTH)]

    def accumulate(s, vt, cs):
        m_old = m_ref[:, cs]
        m_new = jnp.maximum(m_old, jnp.max(s, axis=0, keepdims=True))
        alpha = jnp.exp2(m_old - m_new)
        p = jnp.exp2(s - m_new)
        l_ref[:, cs] = alpha * l_ref[:, cs] + jnp.sum(p, axis=0, keepdims=True)
        acc_ref[:, cs] = alpha * acc_ref[:, cs] + jnp.dot(vt, p.astype(BF16), preferred_element_type=F32)
        m_ref[:, cs] = m_new

    def lat_keys(i):
        return kl_ref[pl.ds(pl.multiple_of(i * TK, TK), TK), :]

    def step(nxt, cur):
        for cs in col_tiles:
            if nxt is not None:
                s_ref[nxt[0], :, cs] = jnp.dot(nxt[1], qqt[:, cs], preferred_element_type=F32)
            if cur is not None:
                accumulate(cur[0][:, cs], cur[1], cs)

    s_ctx = jnp.dot(kc_ref[...], qqt, preferred_element_type=F32)
    if n_lat_steps:
        step((1, lat_keys(0)), (s_ctx, vtc_ref[0]))

        def body(j, carry):
            i = 2 * j
            step((0, lat_keys(i + 1)), (s_ref.at[1], vtl_ref[i]))
            step((1, lat_keys(i + 2)), (s_ref.at[0], vtl_ref[i + 1]))
            return carry
        lax.fori_loop(0, n_lat_steps // 2 - 1, body, 0, unroll=3)
        step((0, lat_keys(n_lat_steps - 1)), (s_ref.at[1], vtl_ref[n_lat_steps - 2]))
        step(None, (s_ref.at[0], vtl_ref[n_lat_steps - 1]))
    else:
        step(None, (s_ctx, vtc_ref[0]))

    out_t = acc_ref[...] / l_ref[...]
    for j in range(n_blocks):
        o_top = out_t[:, (2 * j) * tq:(2 * j + 1) * tq]
        o_bottom = out_t[:, (2 * j + 1) * tq:(2 * j + 2) * tq]
        o_ref[:, j * LANES:(j + 1) * LANES] = jnp.where(top, o_top, o_bottom).T.astype(o_ref.dtype)


def _attention(qt, k, vt, *, batch, seq, ctx_len, latent):
    lat_rows = batch * seq
    n_q = (seq if latent else ctx_len) // TQ
    q_blk0 = 0 if latent else lat_rows // TQ
    ctx_per_slab = TM // ctx_len
    ctx_slab0 = lat_rows // TM
    in_specs = [
        pl.BlockSpec((ATTN_WIDTH, TQ), lambda b, i: (0, q_blk0 + b * n_q + i)),
        pl.BlockSpec((ctx_len, KV_WIDTH), lambda b, i: (lat_rows // ctx_len + b, 0)),
        pl.BlockSpec((1, KV_WIDTH, ctx_len),
                     lambda b, i: (ctx_slab0 + b // ctx_per_slab, 0, b % ctx_per_slab)),
    ]
    operands = [qt, k, vt]
    if latent:
        in_specs += [pl.BlockSpec((seq, KV_WIDTH), lambda b, i: (b, 0)),
                     pl.BlockSpec((seq // TM, KV_WIDTH, TM), lambda b, i: (b, 0, 0))]
        operands += [k, vt]
    cols = N_Q_HEADS * TQ
    scratch = [pltpu.VMEM((1, cols), F32), pltpu.VMEM((1, cols), F32), pltpu.VMEM((KV_WIDTH, cols), F32)]
    if latent:
        assert (seq // TK) % 2 == 0 and seq // TK >= 2
        scratch.append(pltpu.VMEM((2, TK, cols), F32))
    return pl.pallas_call(
        functools.partial(_attn_kernel, n_lat_steps=seq // TK if latent else 0),
        grid=(batch, n_q),
        in_specs=in_specs,
        out_specs=pl.BlockSpec((TQ, ATTN_WIDTH), lambda b, i: (b * n_q + i, 0)),
        out_shape=jax.ShapeDtypeStruct((batch * n_q * TQ, ATTN_WIDTH), BF16),
        scratch_shapes=scratch,
        compiler_params=_cparams(2),
        name="gqa_attention_latent" if latent else "gqa_attention_ctx",
    )(*operands)


def _shift_rows(a, d):
    return pltpu.roll(a, d % a.shape[0], axis=0)


def _pool_group(pe, w, edge):
    def before(a, span):
        shifted = _shift_rows(a, span)
        return shifted if edge is None else jnp.where(edge[0] >= span, shifted, jnp.zeros_like(a))

    def after(a, span):
        shifted = _shift_rows(a, -span)
        return shifted if edge is None else jnp.where(edge[0] + span < edge[1], shifted, jnp.zeros_like(a))

    left = before(pe, 1)
    right = pe
    span = 1
    while 2 * span <= w // 2:
        left = left + before(left, span)
        right = right + after(right, span)
        span *= 2
    if edge is None:
        return (left + right) * (1.0 / w)
    pos, n = edge
    count = jnp.maximum(jnp.minimum(pos + w // 2, n) - jnp.maximum(pos - w // 2, 0), 1)
    return (left + right) / count.astype(F32)


def _rem_static(r, m):
    return r & (m - 1) if m & (m - 1) == 0 else lax.rem(r, m)


def _comb_a_kernel(x_ref, mod_ref, attn_ref, attn_tail_ref, p_ref, pprev_ref, pnext_ref, pw_ref, ps_ref,
                   wo_ref, o_ref, pooled_ref, *, lat_rows, seq, ctx_len, attn_head_tiles):
    tile = pl.program_id(0)
    tiles_per_seq = seq // TM

    def mix(edge):
        attn = jnp.where(tile < attn_head_tiles, attn_ref[...], attn_tail_ref[...])
        mixed = jnp.dot(attn, wo_ref[:ATTN_WIDTH, :], preferred_element_type=F32)
        for gi, w in enumerate(POOL_WINDOWS):
            cols = slice(gi * POOL_GROUP_DIM, (gi + 1) * POOL_GROUP_DIM)
            pe = jnp.concatenate([pprev_ref[:, cols], p_ref[:, cols], pnext_ref[:, cols]], axis=0)
            centred = (_pool_group(pe, w, edge) - pe)[HALO:HALO + TM]
            pooled = jnp.dot(centred.astype(BF16), pw_ref[gi], preferred_element_type=F32) * ps_ref[:, cols]
            pooled_ref[:, cols] = pooled.astype(BF16)
        mixed += jnp.dot(pooled_ref[...], wo_ref[ATTN_WIDTH:, :], preferred_element_type=F32)
        o_ref[...] = x_ref[...] + mod_ref[0][5:6] * mixed

    seq_tile = lax.rem(tile, tiles_per_seq)
    at_edge = (tile >= lat_rows // TM) | (seq_tile == 0) | (seq_tile == tiles_per_seq - 1)

    @pl.when(at_edge)
    def _():
        rows = TM + 2 * HALO
        r = tile * TM - HALO + lax.broadcasted_iota(jnp.int32, (rows, POOL_GROUP_DIM), 0)
        is_lat = r < lat_rows
        n = jnp.where(is_lat, seq, ctx_len)
        pos = jnp.where(is_lat, _rem_static(r, seq), _rem_static(r - lat_rows, ctx_len))
        mix((pos, n))

    @pl.when(jnp.logical_not(at_edge))
    def _():
        mix(None)


def _combine_a(xs, mods, attn, attn_tail, p, pool_w, pool_scale, w_out, *, n_tiles, group_of_tile,
               lat_rows, seq, ctx_len):
    d = xs.shape[1]
    pool_width = p.shape[1]
    halo_per_tile = TM // HALO
    last_halo_blk = p.shape[0] // HALO - 1
    head_tiles = attn.shape[0] // TM
    return pl.pallas_call(
        functools.partial(_comb_a_kernel, lat_rows=lat_rows, seq=seq, ctx_len=ctx_len,
                          attn_head_tiles=head_tiles),
        grid=(n_tiles,),
        in_specs=[
            pl.BlockSpec((TM, d), lambda t: (t, 0)),
            pl.BlockSpec((1, N_MOD, d), lambda t: (group_of_tile(t), 0, 0)),
            pl.BlockSpec((TM, ATTN_WIDTH), lambda t: (jnp.minimum(t, head_tiles - 1), 0)),
            pl.BlockSpec((TM, ATTN_WIDTH), lambda t: (jnp.maximum(t - head_tiles, 0), 0)),
            pl.BlockSpec((TM, pool_width), lambda t: (t, 0)),
            pl.BlockSpec((HALO, pool_width), lambda t: (jnp.maximum(t * halo_per_tile - 1, 0), 0)),
            pl.BlockSpec((HALO, pool_width),
                         lambda t: (jnp.minimum((t + 1) * halo_per_tile, last_halo_blk), 0)),
            _resident(pool_w.shape),
            pl.BlockSpec((1, pool_width), lambda t: (0, 0)),
            _resident(w_out.shape),
        ],
        out_specs=pl.BlockSpec((TM, d), lambda t: (t, 0)),
        out_shape=jax.ShapeDtypeStruct((n_tiles * TM, d), F32),
        scratch_shapes=[pltpu.VMEM((TM, pool_width), BF16)],
        compiler_params=_cparams(1),
        name="mixer_a_combine",
    )(xs, mods, attn, attn_tail, p, p, p, pool_w, pool_scale, w_out)


def _gelu_tanh(x):
    return 0.5 * x * (1.0 + jnp.tanh(np.sqrt(2.0 / np.pi).astype(np.float32) * (x + 0.044715 * (x * x * x))))


def _mixer_c_kernel(x_ref, mod_ref, g_ref, wi_ref, vg_ref, wsp_ref, bsp_ref, wo_ref, o_ref, gated_ref):
    x = x_ref[...]
    mod = mod_ref[0]
    width = wo_ref.shape[0]
    h = _modulate(x, g_ref[...], mod[3:4], mod[4:5]).astype(BF16)
    z = _gelu_tanh(jnp.dot(h, wi_ref[...], preferred_element_type=F32))
    u = z[:, :width]
    v = z[:, width:]
    v = (v * lax.rsqrt(jnp.mean(v * v, axis=-1, keepdims=True) + EPS) * vg_ref[...]).astype(BF16)
    bias = bsp_ref[...]
    for g in range(GMLP_GROUPS):
        cols = slice(g * CHUNK, (g + 1) * CHUNK)
        b_col = bias[:, g:g + 1]
        for c in range(TM // CHUNK):
            rows = slice(c * CHUNK, (c + 1) * CHUNK)
            sv = jnp.dot(wsp_ref[g], v[rows, cols], preferred_element_type=F32) + b_col
            gated_ref[rows, cols] = (u[rows, cols] * sv).astype(BF16)
    o_ref[...] = x + mod[5:6] * jnp.dot(gated_ref[...], wo_ref[...], preferred_element_type=F32)


def _mixer_c(xs, mods, g, w_in, v_g, w_sp, b_sp_cols, w_out, *, n_tiles, group_of_tile):
    d = xs.shape[1]
    width = w_out.shape[0]
    return pl.pallas_call(
        _mixer_c_kernel,
        grid=(n_tiles,),
        in_specs=[
            pl.BlockSpec((TM, d), lambda t: (t, 0)),
            pl.BlockSpec((1, N_MOD, d), lambda t: (group_of_tile(t), 0, 0)),
            pl.BlockSpec((1, d), lambda t: (0, 0)),
            _resident(w_in.shape),
            pl.BlockSpec((1, width), lambda t: (0, 0)),
            _resident(w_sp.shape),
            pl.BlockSpec(b_sp_cols.shape, lambda t: (0, 0)),
            _resident(w_out.shape),
        ],
        out_specs=pl.BlockSpec((TM, d), lambda t: (t, 0)),
        out_shape=jax.ShapeDtypeStruct((n_tiles * TM, d), F32),
        scratch_shapes=[pltpu.VMEM((TM, width), BF16)],
        compiler_params=_cparams(1),
        name="mixer_c_gmlp",
    )(xs, mods, g, w_in, v_g, w_sp, b_sp_cols, w_out)


def _rope_tables(seq):
    half = HEAD_DIM // 2
    inv_freq = ROPE_THETA ** (-jnp.arange(0, half, 2, dtype=F32) / half)

    def parts(n, row_axis):
        ang = jnp.arange(n).astype(F32)[:, None] * inv_freq[None, :]
        cos, sin, zero = jnp.cos(ang), jnp.sin(ang), jnp.zeros_like(ang)
        if row_axis:
            heads = ([cos, cos, zero, zero], [zero, sin, zero, zero], [-sin, zero, zero, zero])
        else:
            heads = ([zero, zero, cos, cos], [zero, zero, zero, sin], [zero, zero, -sin, zero])
        return jnp.concatenate([blk for head in heads for blk in head * (LANES // HEAD_DIM)], axis=1)

    return parts(seq // GRID_W, True), parts(GRID_W, False)


def _head_block_order():
    order = []
    for j in range(Q_GROUP):
        order += [j, j + Q_GROUP]
    return np.concatenate([np.arange(h * HEAD_DIM, (h + 1) * HEAD_DIM) for h in order])


def kernel(x, c, ctx, c_ctx, w_mod, b_mod, norm_g, ffn_w13, ffn_w2, w_in_a, qk_norm_g, pool_w,
           pool_scale, w_out_a, w_in_c, v_norm_g, w_sp, b_sp, w_out_c):
    batch, seq, d = x.shape
    ctx_len = ctx.shape[1]
    depth = w_mod.shape[0]
    d_ff = ffn_w2.shape[2]
    lat_rows = batch * seq
    ctx_rows = batch * ctx_len
    assert seq % TM == 0 and ctx_rows % TM == 0 and seq % TK == 0 and ctx_len % TQ == 0
    assert ctx_len % HALO == 0 and lat_rows % ctx_len == 0 and d_ff % FF_CHUNK == 0
    assert TK == TM and TM % ctx_len == 0
    assert seq % GRID_W == 0 and TM % (GRID_W * SUBLANES) == 0
    lat_tiles = lat_rows // TM
    all_tiles = lat_tiles + ctx_rows // TM
    tiles_per_batch = seq // TM
    group_of_tile = lambda t: jnp.minimum(t // tiles_per_batch, batch)

    c_rows = jnp.concatenate([c, c_ctx[None, :], jnp.zeros((SUBLANES - batch - 1, d), F32)], axis=0)
    mods_all = _modulation(c_rows, w_mod, b_mod)
    mods_all = jnp.transpose(mods_all[:, :, :batch + 1], (0, 2, 1, 3))

    head_cols = _head_block_order()
    rope_by_row, rope_by_col = _rope_tables(seq)
    head_mean = jnp.asarray(np.kron(np.eye(LANES // HEAD_DIM), np.full((HEAD_DIM, HEAD_DIM), 1.0 / HEAD_DIM)), BF16)

    xs = x.reshape(lat_rows, d)
    tail = ctx.reshape(ctx_rows, d)
    for i in range(depth):
        mods = mods_all[i]
        reads_ctx = i % 2 == 0
        later_reads_ctx = any(j % 2 == 0 for j in range(i + 1, depth))
        n1 = all_tiles if (reads_ctx or later_reads_ctx) else lat_tiles
        n2 = all_tiles if later_reads_ctx else lat_tiles
        common = dict(group_of_tile=group_of_tile)

        xs = _half_ffn(xs, mods, norm_g[i, 0][None], ffn_w13, ffn_w2, layer=i, half=0, n_tiles=n1,
                       tail=tail if i == 0 else None, **common)

        if reads_ctx:
            e = i // 2
            w_in = w_in_a[e]
            w_in = jnp.concatenate([w_in[:, :ATTN_WIDTH][:, head_cols], w_in[:, ATTN_WIDTH:]], axis=1).astype(BF16)
            w_out = jnp.concatenate([w_out_a[e][:ATTN_WIDTH][head_cols], w_out_a[e][ATTN_WIDTH:]], axis=0).astype(BF16)
            gq = jnp.tile(qk_norm_g[e, 0], LANES // HEAD_DIM)[None]
            gk = jnp.tile(qk_norm_g[e, 1], LANES // HEAD_DIM)[None]
            qt, k, vt, p = _project_a(xs, mods, norm_g[i, 1][None], w_in, gq, gk, head_mean,
                                      rope_by_row, rope_by_col, n_tiles=n1, lat_tiles=lat_tiles,
                                      tiles_per_batch=tiles_per_batch, **common)
            attn = _attention(qt, k, vt, batch=batch, seq=seq, ctx_len=ctx_len, latent=True)
            attn_c = (_attention(qt, k, vt, batch=batch, seq=seq, ctx_len=ctx_len, latent=False)
                      if later_reads_ctx else attn)
            xs = _combine_a(xs, mods, attn, attn_c, p, pool_w[e].astype(BF16), pool_scale[e][None], w_out,
                            n_tiles=n2, lat_rows=lat_rows, seq=seq, ctx_len=ctx_len, **common)
        else:
            o = i // 2
            xs = _mixer_c(xs, mods, norm_g[i, 1][None], w_in_c[o].astype(BF16), v_norm_g[o][None],
                          w_sp[o].astype(BF16), b_sp[o].T, w_out_c[o].astype(BF16), n_tiles=n2, **common)

        xs = _half_ffn(xs, mods, norm_g[i, 2][None], ffn_w13, ffn_w2, layer=i, half=1, n_tiles=n2, **common)
    return xs[:lat_rows].reshape(batch, seq, d)
```

```python
import functools

import jax
import jax.numpy as jnp
import numpy as np
from jax import lax
from jax.experimental import pallas as pl
from jax.experimental.pallas import tpu as pltpu

F32 = jnp.float32
BF16 = jnp.bfloat16

GRID_W = 64
HEAD_DIM = 64
N_Q_HEADS = 8
N_KV_HEADS = 2
Q_GROUP = N_Q_HEADS // N_KV_HEADS
ATTN_WIDTH = N_Q_HEADS * HEAD_DIM
KV_WIDTH = N_KV_HEADS * HEAD_DIM
ATTN_SCALE = HEAD_DIM ** -0.5
Q_SCALE_LOG2 = float(ATTN_SCALE * np.log2(np.e))
ROPE_THETA = 10000.0
POOL_WINDOWS = (2, 4, 8, 16)
POOL_GROUP_DIM = 128
CHUNK = 128
GMLP_GROUPS = 8
N_MOD = 9
EPS = 1e-6

LANES = 128
MXU_WIDTH = 256
VT_HEAD_ROWS = HEAD_DIM + 16
VT_ROWS = N_KV_HEADS * VT_HEAD_ROWS
SUBLANES = 8
VMEM_LIMIT_BYTES = 56 * 1024 * 1024

TM = 512
FF_CHUNK = 256
TQ = 256
TK = 512
HALO = 8


def _cparams(n_axes):
    return pltpu.CompilerParams(
        dimension_semantics=("arbitrary",) * n_axes,
        vmem_limit_bytes=VMEM_LIMIT_BYTES)


def _resident(shape):
    zeros = (0,) * len(shape)
    return pl.BlockSpec(shape, lambda *_: zeros, pipeline_mode=pl.Buffered(1))


def _modulate(x, g, shift, scale):
    ms = jnp.mean(x * x, axis=-1, keepdims=True)
    return x * lax.rsqrt(ms + EPS) * (g * (1.0 + scale)) + shift


def _mod_kernel(c_ref, w_ref, b_ref, o_ref):
    c = c_ref[...]
    s = (c * jax.nn.sigmoid(c)).astype(BF16)
    o_ref[0, 0] = jnp.dot(s, w_ref[0].astype(BF16), preferred_element_type=F32) + b_ref[0, 0]


def _modulation(c_rows, w_mod, b_mod):
    depth, d, _ = w_mod.shape
    rows = c_rows.shape[0]
    out = pl.pallas_call(
        _mod_kernel,
        grid=(depth, N_MOD),
        in_specs=[
            pl.BlockSpec((rows, d), lambda i, j: (0, 0)),
            pl.BlockSpec((1, d, d), lambda i, j: (i, 0, j)),
            pl.BlockSpec((1, 1, 1, d), lambda i, j: (i, j, 0, 0)),
        ],
        out_specs=pl.BlockSpec((1, 1, rows, d), lambda i, j: (i, j, 0, 0)),
        out_shape=jax.ShapeDtypeStruct((depth, N_MOD, rows, d), F32),
        compiler_params=_cparams(2),
        name="adaln_modulation",
    )(c_rows, w_mod, b_mod.reshape(depth, N_MOD, 1, d))
    return out


def _ffn_kernel(*refs, k0, head_tiles, layer, half):
    if head_tiles is None:
        x_ref, mod_ref, g_ref, w13_hbm, w2_hbm, o_ref, w13_ref, w2_ref, sem, h_ref, act_ref, acc_ref = refs
        x = x_ref[...]
    else:
        (x_ref, tail_ref, mod_ref, g_ref, w13_hbm, w2_hbm, o_ref, w13_ref, w2_ref, sem,
         h_ref, act_ref, acc_ref) = refs
        x = jnp.where(pl.program_id(0) < head_tiles, x_ref[...], tail_ref[...])
    mod = mod_ref[0]
    d_ff = w2_ref.shape[0]
    n_chunks = d_ff // FF_CHUNK
    first_step = pl.program_id(0) == 0

    def chunk(c, offset=0):
        start = c * FF_CHUNK + offset
        return pl.ds(start if isinstance(c, int) else pl.multiple_of(start, LANES), FF_CHUNK)

    def up_copy(c, part):
        cols = chunk(c, part * d_ff)
        return pltpu.make_async_copy(w13_hbm.at[layer, half, :, cols], w13_ref.at[:, cols], sem.at[part, c])

    def down_copy(c):
        return pltpu.make_async_copy(w2_hbm.at[layer, half, chunk(c), :], w2_ref.at[chunk(c), :], sem.at[2, c])

    def half_step(weights_in_flight):
        h_ref[...] = _modulate(x, g_ref[...], mod[k0:k0 + 1], mod[k0 + 1:k0 + 2]).astype(BF16)

        def gate(slot, c):
            if weights_in_flight:
                up_copy(c, 0).wait()
                up_copy(c, 1).wait()
            h = h_ref[...]
            a = jnp.dot(h, w13_ref[:, chunk(c)].astype(BF16), preferred_element_type=F32)
            b = jnp.dot(h, w13_ref[:, chunk(c, d_ff)].astype(BF16), preferred_element_type=F32)
            act_ref[slot] = (a * jax.nn.sigmoid(a) * b).astype(BF16)

        def down(slot, c, first=False):
            if weights_in_flight:
                down_copy(c).wait()
            part = jnp.dot(act_ref[slot], w2_ref[chunk(c), :].astype(BF16), preferred_element_type=F32)
            acc_ref[...] = part if first else acc_ref[...] + part

        assert n_chunks >= 2
        gate(0, 0)
        gate(1, 1)
        down(0, 0, first=True)

        def body(j, carry):
            c = 2 * j
            gate(0, c + 2)
            down(1, c + 1)
            gate(1, c + 3)
            down(0, c + 2)
            return carry

        lax.fori_loop(0, (n_chunks - 2) // 2, body, 0, unroll=True)
        if n_chunks % 2:
            gate(0, n_chunks - 1)
            down(1, n_chunks - 2)
            down(0, n_chunks - 1)
        else:
            down(1, n_chunks - 1)
        o_ref[...] = x + 0.5 * mod[k0 + 2:k0 + 3] * acc_ref[...]

    @pl.when(first_step)
    def _():
        for c in range(n_chunks):
            up_copy(c, 0).start()
            up_copy(c, 1).start()
            down_copy(c).start()
        half_step(weights_in_flight=True)

    @pl.when(jnp.logical_not(first_step))
    def _():
        half_step(weights_in_flight=False)


def _half_ffn(xs, mods, g, w13, w2, *, layer, half, n_tiles, group_of_tile, tail=None):
    d = xs.shape[1]
    d_ff = w2.shape[2]
    k0 = 6 * half
    in_hbm = pl.BlockSpec(memory_space=pl.ANY)
    if tail is None:
        head_tiles = None
        stream, stream_specs = [xs], [pl.BlockSpec((TM, d), lambda t: (t, 0))]
    else:
        head_tiles = xs.shape[0] // TM
        stream = [xs, tail]
        stream_specs = [pl.BlockSpec((TM, d), lambda t: (jnp.minimum(t, head_tiles - 1), 0)),
                        pl.BlockSpec((TM, d), lambda t: (jnp.maximum(t - head_tiles, 0), 0))]
    return pl.pallas_call(
        functools.partial(_ffn_kernel, k0=k0, head_tiles=head_tiles, layer=layer, half=half),
        grid=(n_tiles,),
        in_specs=stream_specs + [
            pl.BlockSpec((1, N_MOD, d), lambda t: (group_of_tile(t), 0, 0)),
            pl.BlockSpec((1, d), lambda t: (0, 0)),
            in_hbm,
            in_hbm,
        ],
        out_specs=pl.BlockSpec((TM, d), lambda t: (t, 0)),
        out_shape=jax.ShapeDtypeStruct((n_tiles * TM, d), F32),
        scratch_shapes=[pltpu.VMEM((d, 2 * d_ff), F32), pltpu.VMEM((d_ff, d), F32),
                        pltpu.SemaphoreType.DMA((3, d_ff // FF_CHUNK)),
                        pltpu.VMEM((TM, d), BF16), pltpu.VMEM((2, TM, FF_CHUNK), BF16),
                        pltpu.VMEM((TM, d), F32)],
        compiler_params=_cparams(1),
        name="half_ffn",
    )(*stream, mods, g, w13, w2)


def _rope(x, cos, sin_hi, sin_lo):
    return (x * cos + pltpu.roll(x, 16, axis=1) * sin_hi
            + pltpu.roll(x, LANES - 16, axis=1) * sin_lo)


def _proj_a_kernel(x_ref, mod_ref, g_ref, w_ref, gq_ref, gk_ref, hsum_ref,
                   rope_row_ref, rope_col_ref, qt_ref, k_ref, vt_ref, p_ref, *, lat_tiles):
    mod = mod_ref[0]
    h = _modulate(x_ref[...], g_ref[...], mod[3:4], mod[4:5]).astype(BF16)
    z = jnp.dot(h, w_ref[...], preferred_element_type=F32)

    by_row, by_col = rope_row_ref[...], rope_col_ref[...]
    rope = jnp.concatenate([jnp.broadcast_to(by_row[r:r + 1, :], by_col.shape) + by_col
                            for r in range(TM // GRID_W)], axis=0)
    is_ctx = pl.program_id(0) >= lat_tiles
    cos = jnp.where(is_ctx, 1.0, rope[:, :LANES])
    shi = jnp.where(is_ctx, 0.0, rope[:, LANES:2 * LANES])
    slo = jnp.where(is_ctx, 0.0, rope[:, 2 * LANES:])
    hsum = hsum_ref[...]

    def norm_rope(zb, gain):
        ms = jnp.dot((zb * zb).astype(BF16), hsum, preferred_element_type=F32)
        return _rope(zb * lax.rsqrt(ms + EPS) * gain, cos, shi, slo)

    for j in range(ATTN_WIDTH // LANES):
        qb = norm_rope(z[:, j * LANES:(j + 1) * LANES], gq_ref[...])
        qt_ref[j * LANES:(j + 1) * LANES, :] = (qb * Q_SCALE_LOG2).T.astype(BF16)
    k_ref[...] = norm_rope(z[:, ATTN_WIDTH:ATTN_WIDTH + KV_WIDTH], gk_ref[...]).astype(BF16)
    v_t = z[:, ATTN_WIDTH + KV_WIDTH:ATTN_WIDTH + 2 * KV_WIDTH].T.astype(BF16)
    ones_row = (lax.broadcasted_iota(jnp.int32, (VT_HEAD_ROWS - HEAD_DIM, TM), 0) == 0).astype(BF16)
    vt_ref[0] = jnp.concatenate([v_t[:HEAD_DIM], ones_row, v_t[HEAD_DIM:], ones_row], axis=0)
    p_ref[...] = z[:, ATTN_WIDTH + 2 * KV_WIDTH:]


def _project_a(xs, mods, g, w_in, gq, gk, hsum, rope_by_row, rope_by_col, *, n_tiles, group_of_tile,
               lat_tiles, tiles_per_batch):
    d = xs.shape[1]
    nt = n_tiles * TM
    grid_rows = TM // GRID_W
    pool_width = w_in.shape[1] - ATTN_WIDTH - 2 * KV_WIDTH
    row_tile = lambda w: pl.BlockSpec((TM, w), lambda t: (t, 0))
    const2 = lambda shape: pl.BlockSpec(shape, lambda t: (0, 0))
    return pl.pallas_call(
        functools.partial(_proj_a_kernel, lat_tiles=lat_tiles),
        grid=(n_tiles,),
        in_specs=[
            row_tile(d),
            pl.BlockSpec((1, N_MOD, d), lambda t: (group_of_tile(t), 0, 0)),
            const2((1, d)),
            _resident(w_in.shape),
            const2((1, LANES)), const2((1, LANES)), const2((LANES, LANES)),
            pl.BlockSpec((grid_rows, 3 * LANES), lambda t: (t % tiles_per_batch, 0)),
            const2(rope_by_col.shape),
        ],
        out_specs=[
            pl.BlockSpec((ATTN_WIDTH, TM), lambda t: (0, t)),
            row_tile(KV_WIDTH),
            pl.BlockSpec((1, VT_ROWS, TM), lambda t: (t, 0, 0)),
            row_tile(pool_width),
        ],
        out_shape=[
            jax.ShapeDtypeStruct((ATTN_WIDTH, nt), BF16),
            jax.ShapeDtypeStruct((nt, KV_WIDTH), BF16),
            jax.ShapeDtypeStruct((n_tiles, VT_ROWS, TM), BF16),
            jax.ShapeDtypeStruct((nt, pool_width), F32),
        ],
        compiler_params=_cparams(1),
        name="mixer_a_project",
    )(xs, mods, g, w_in, gq, gk, hsum, rope_by_row, rope_by_col)


def _attn_kernel(*refs, n_lat_steps):
    if n_lat_steps:
        qt_ref, kc_ref, vtc_ref, kl_ref, vtl_ref, o_ref, m_ref, acc_ref, s_ref = refs
    else:
        qt_ref, kc_ref, vtc_ref, o_ref, m_ref, acc_ref = refs
    tq = qt_ref.shape[1]
    n_blocks = ATTN_WIDTH // LANES
    top = lax.broadcasted_iota(jnp.int32, (LANES, tq), 0) < HEAD_DIM
    pieces = []
    for j in range(n_blocks):
        blk = qt_ref[j * LANES:(j + 1) * LANES, :]
        zero = jnp.zeros_like(blk)
        pieces += [jnp.where(top, blk, zero), jnp.where(top, zero, blk)]
    qqt = jnp.concatenate(pieces, axis=1)

    m_ref[...] = jnp.full_like(m_ref, -jnp.inf)
    acc_ref[...] = jnp.zeros_like(acc_ref)

    assert tq % MXU_WIDTH == 0
    col_tiles = [(slice(ct * MXU_WIDTH, (ct + 1) * MXU_WIDTH), (ct * MXU_WIDTH // tq) % N_KV_HEADS)
                 for ct in range(qqt.shape[1] // MXU_WIDTH)]

    def accumulate(s, vt, cs, kv):
        m_old = m_ref[:, cs]
        m_new = jnp.maximum(m_old, jnp.max(s, axis=0, keepdims=True))
        alpha = jnp.exp2(m_old - m_new)
        p = jnp.exp2(s - m_new).astype(BF16)
        pv = jnp.dot(vt[kv * VT_HEAD_ROWS:(kv + 1) * VT_HEAD_ROWS, :], p, preferred_element_type=F32)
        acc_ref[:, cs] = alpha * acc_ref[:, cs] + pv
        m_ref[:, cs] = m_new

    def lat_keys(i):
        return kl_ref[pl.ds(pl.multiple_of(i * TK, TK), TK), :]

    def step(nxt, cur):
        for cs, kv in col_tiles:
            if nxt is not None:
                s_ref[nxt[0], :, cs] = jnp.dot(nxt[1], qqt[:, cs], preferred_element_type=F32)
            if cur is not None:
                accumulate(cur[0][:, cs], cur[1], cs, kv)

    s_ctx = jnp.dot(kc_ref[...], qqt, preferred_element_type=F32)
    if n_lat_steps:
        step((1, lat_keys(0)), (s_ctx, vtc_ref[0]))

        def body(j, carry):
            i = 2 * j
            step((0, lat_keys(i + 1)), (s_ref.at[1], vtl_ref[i]))
            step((1, lat_keys(i + 2)), (s_ref.at[0], vtl_ref[i + 1]))
            return carry
        lax.fori_loop(0, n_lat_steps // 2 - 1, body, 0, unroll=3)
        step((0, lat_keys(n_lat_steps - 1)), (s_ref.at[1], vtl_ref[n_lat_steps - 2]))
        step(None, (s_ref.at[0], vtl_ref[n_lat_steps - 1]))
    else:
        step(None, (s_ctx, vtc_ref[0]))

    out_t = acc_ref[:HEAD_DIM, :] / acc_ref[HEAD_DIM:HEAD_DIM + 1, :]
    for j in range(n_blocks):
        blk = jnp.concatenate([out_t[:, (2 * j) * tq:(2 * j + 1) * tq],
                               out_t[:, (2 * j + 1) * tq:(2 * j + 2) * tq]], axis=0)
        o_ref[:, j * LANES:(j + 1) * LANES] = blk.T.astype(o_ref.dtype)


def _attention(qt, k, vt, *, batch, seq, ctx_len, latent):
    lat_rows = batch * seq
    n_q = (seq if latent else ctx_len) // TQ
    q_blk0 = 0 if latent else lat_rows // TQ
    ctx_per_slab = TM // ctx_len
    ctx_slab0 = lat_rows // TM
    in_specs = [
        pl.BlockSpec((ATTN_WIDTH, TQ), lambda b, i: (0, q_blk0 + b * n_q + i)),
        pl.BlockSpec((ctx_len, KV_WIDTH), lambda b, i: (lat_rows // ctx_len + b, 0)),
        pl.BlockSpec((1, VT_ROWS, ctx_len),
                     lambda b, i: (ctx_slab0 + b // ctx_per_slab, 0, b % ctx_per_slab)),
    ]
    operands = [qt, k, vt]
    if latent:
        in_specs += [pl.BlockSpec((seq, KV_WIDTH), lambda b, i: (b, 0)),
                     pl.BlockSpec((seq // TM, VT_ROWS, TM), lambda b, i: (b, 0, 0))]
        operands += [k, vt]
    cols = N_Q_HEADS * TQ
    scratch = [pltpu.VMEM((1, cols), F32), pltpu.VMEM((VT_HEAD_ROWS, cols), F32)]
    if latent:
        assert (seq // TK) % 2 == 0 and seq // TK >= 2
        scratch.append(pltpu.VMEM((2, TK, cols), F32))
    return pl.pallas_call(
        functools.partial(_attn_kernel, n_lat_steps=seq // TK if latent else 0),
        grid=(batch, n_q),
        in_specs=in_specs,
        out_specs=pl.BlockSpec((TQ, ATTN_WIDTH), lambda b, i: (b * n_q + i, 0)),
        out_shape=jax.ShapeDtypeStruct((batch * n_q * TQ, ATTN_WIDTH), BF16),
        scratch_shapes=scratch,
        compiler_params=_cparams(2),
        name="gqa_attention_latent" if latent else "gqa_attention_ctx",
    )(*operands)


def _shift_rows(a, d):
    return pltpu.roll(a, d % a.shape[0], axis=0)


def _pool_group(pe, w, edge):
    def before(a, span):
        shifted = _shift_rows(a, span)
        return shifted if edge is None else jnp.where(edge[0] >= span, shifted, jnp.zeros_like(a))

    def after(a, span):
        shifted = _shift_rows(a, -span)
        return shifted if edge is None else jnp.where(edge[0] + span < edge[1], shifted, jnp.zeros_like(a))

    left = before(pe, 1)
    right = pe
    span = 1
    while 2 * span <= w // 2:
        left = left + before(left, span)
        right = right + after(right, span)
        span *= 2
    if edge is None:
        return (left + right) * (1.0 / w)
    pos, n = edge
    count = jnp.maximum(jnp.minimum(pos + w // 2, n) - jnp.maximum(pos - w // 2, 0), 1)
    return (left + right) / count.astype(F32)


def _rem_static(r, m):
    return r & (m - 1) if m & (m - 1) == 0 else lax.rem(r, m)


def _comb_a_kernel(x_ref, mod_ref, attn_ref, attn_tail_ref, p_ref, pprev_ref, pnext_ref, pw_ref, ps_ref,
                   wo_ref, o_ref, pooled_ref, *, lat_rows, seq, ctx_len, attn_head_tiles):
    tile = pl.program_id(0)
    tiles_per_seq = seq // TM

    def mix(edge):
        attn = jnp.where(tile < attn_head_tiles, attn_ref[...], attn_tail_ref[...])
        mixed = jnp.dot(attn, wo_ref[:ATTN_WIDTH, :], preferred_element_type=F32)
        for gi, w in enumerate(POOL_WINDOWS):
            cols = slice(gi * POOL_GROUP_DIM, (gi + 1) * POOL_GROUP_DIM)
            pe = jnp.concatenate([pprev_ref[:, cols], p_ref[:, cols], pnext_ref[:, cols]], axis=0)
            centred = (_pool_group(pe, w, edge) - pe)[HALO:HALO + TM]
            pooled = jnp.dot(centred.astype(BF16), pw_ref[gi], preferred_element_type=F32) * ps_ref[:, cols]
            pooled_ref[:, cols] = pooled.astype(BF16)
        mixed += jnp.dot(pooled_ref[...], wo_ref[ATTN_WIDTH:, :], preferred_element_type=F32)
        o_ref[...] = x_ref[...] + mod_ref[0][5:6] * mixed

    seq_tile = lax.rem(tile, tiles_per_seq)
    at_edge = (tile >= lat_rows // TM) | (seq_tile == 0) | (seq_tile == tiles_per_seq - 1)

    @pl.when(at_edge)
    def _():
        rows = TM + 2 * HALO
        r = tile * TM - HALO + lax.broadcasted_iota(jnp.int32, (rows, POOL_GROUP_DIM), 0)
        is_lat = r < lat_rows
        n = jnp.where(is_lat, seq, ctx_len)
        pos = jnp.where(is_lat, _rem_static(r, seq), _rem_static(r - lat_rows, ctx_len))
        mix((pos, n))

    @pl.when(jnp.logical_not(at_edge))
    def _():
        mix(None)


def _combine_a(xs, mods, attn, attn_tail, p, pool_w, pool_scale, w_out, *, n_tiles, group_of_tile,
               lat_rows, seq, ctx_len):
    d = xs.shape[1]
    pool_width = p.shape[1]
    halo_per_tile = TM // HALO
    last_halo_blk = p.shape[0] // HALO - 1
    head_tiles = attn.shape[0] // TM
    return pl.pallas_call(
        functools.partial(_comb_a_kernel, lat_rows=lat_rows, seq=seq, ctx_len=ctx_len,
                          attn_head_tiles=head_tiles),
        grid=(n_tiles,),
        in_specs=[
            pl.BlockSpec((TM, d), lambda t: (t, 0)),
            pl.BlockSpec((1, N_MOD, d), lambda t: (group_of_tile(t), 0, 0)),
            pl.BlockSpec((TM, ATTN_WIDTH), lambda t: (jnp.minimum(t, head_tiles - 1), 0)),
            pl.BlockSpec((TM, ATTN_WIDTH), lambda t: (jnp.maximum(t - head_tiles, 0), 0)),
            pl.BlockSpec((TM, pool_width), lambda t: (t, 0)),
            pl.BlockSpec((HALO, pool_width), lambda t: (jnp.maximum(t * halo_per_tile - 1, 0), 0)),
            pl.BlockSpec((HALO, pool_width),
                         lambda t: (jnp.minimum((t + 1) * halo_per_tile, last_halo_blk), 0)),
            _resident(pool_w.shape),
            pl.BlockSpec((1, pool_width), lambda t: (0, 0)),
            _resident(w_out.shape),
        ],
        out_specs=pl.BlockSpec((TM, d), lambda t: (t, 0)),
        out_shape=jax.ShapeDtypeStruct((n_tiles * TM, d), F32),
        scratch_shapes=[pltpu.VMEM((TM, pool_width), BF16)],
        compiler_params=_cparams(1),
        name="mixer_a_combine",
    )(xs, mods, attn, attn_tail, p, p, p, pool_w, pool_scale, w_out)


def _gelu_tanh(x):
    return 0.5 * x * (1.0 + jnp.tanh(np.sqrt(2.0 / np.pi).astype(np.float32) * (x + 0.044715 * (x * x * x))))


def _mixer_c_kernel(x_ref, mod_ref, g_ref, wi_ref, vg_ref, wsp_ref, bsp_ref, wo_ref, o_ref, gated_ref):
    x = x_ref[...]
    mod = mod_ref[0]
    width = wo_ref.shape[0]
    h = _modulate(x, g_ref[...], mod[3:4], mod[4:5]).astype(BF16)
    z = _gelu_tanh(jnp.dot(h, wi_ref[...], preferred_element_type=F32))
    u = z[:, :width]
    v = z[:, width:]
    v = (v * lax.rsqrt(jnp.mean(v * v, axis=-1, keepdims=True) + EPS) * vg_ref[...]).astype(BF16)
    bias = bsp_ref[...]
    for g in range(GMLP_GROUPS):
        cols = slice(g * CHUNK, (g + 1) * CHUNK)
        b_col = bias[:, g:g + 1]
        for c in range(TM // CHUNK):
            rows = slice(c * CHUNK, (c + 1) * CHUNK)
            sv = jnp.dot(wsp_ref[g], v[rows, cols], preferred_element_type=F32) + b_col
            gated_ref[rows, cols] = (u[rows, cols] * sv).astype(BF16)
    o_ref[...] = x + mod[5:6] * jnp.dot(gated_ref[...], wo_ref[...], preferred_element_type=F32)


def _mixer_c(xs, mods, g, w_in, v_g, w_sp, b_sp_cols, w_out, *, n_tiles, group_of_tile):
    d = xs.shape[1]
    width = w_out.shape[0]
    return pl.pallas_call(
        _mixer_c_kernel,
        grid=(n_tiles,),
        in_specs=[
            pl.BlockSpec((TM, d), lambda t: (t, 0)),
            pl.BlockSpec((1, N_MOD, d), lambda t: (group_of_tile(t), 0, 0)),
            pl.BlockSpec((1, d), lambda t: (0, 0)),
            _resident(w_in.shape),
            pl.BlockSpec((1, width), lambda t: (0, 0)),
            _resident(w_sp.shape),
            pl.BlockSpec(b_sp_cols.shape, lambda t: (0, 0)),
            _resident(w_out.shape),
        ],
        out_specs=pl.BlockSpec((TM, d), lambda t: (t, 0)),
        out_shape=jax.ShapeDtypeStruct((n_tiles * TM, d), F32),
        scratch_shapes=[pltpu.VMEM((TM, width), BF16)],
        compiler_params=_cparams(1),
        name="mixer_c_gmlp",
    )(xs, mods, g, w_in, v_g, w_sp, b_sp_cols, w_out)


def _rope_tables(seq):
    half = HEAD_DIM // 2
    inv_freq = ROPE_THETA ** (-jnp.arange(0, half, 2, dtype=F32) / half)

    def parts(n, row_axis):
        ang = jnp.arange(n).astype(F32)[:, None] * inv_freq[None, :]
        cos, sin, zero = jnp.cos(ang), jnp.sin(ang), jnp.zeros_like(ang)
        if row_axis:
            heads = ([cos, cos, zero, zero], [zero, sin, zero, zero], [-sin, zero, zero, zero])
        else:
            heads = ([zero, zero, cos, cos], [zero, zero, zero, sin], [zero, zero, -sin, zero])
        return jnp.concatenate([blk for head in heads for blk in head * (LANES // HEAD_DIM)], axis=1)

    return parts(seq // GRID_W, True), parts(GRID_W, False)


def _head_block_order():
    order = []
    for j in range(Q_GROUP):
        order += [j, j + Q_GROUP]
    return np.concatenate([np.arange(h * HEAD_DIM, (h + 1) * HEAD_DIM) for h in order])


def kernel(x, c, ctx, c_ctx, w_mod, b_mod, norm_g, ffn_w13, ffn_w2, w_in_a, qk_norm_g, pool_w,
           pool_scale, w_out_a, w_in_c, v_norm_g, w_sp, b_sp, w_out_c):
    batch, seq, d = x.shape
    ctx_len = ctx.shape[1]
    depth = w_mod.shape[0]
    d_ff = ffn_w2.shape[2]
    lat_rows = batch * seq
    ctx_rows = batch * ctx_len
    assert seq % TM == 0 and ctx_rows % TM == 0 and seq % TK == 0 and ctx_len % TQ == 0
    assert ctx_len % HALO == 0 and lat_rows % ctx_len == 0 and d_ff % FF_CHUNK == 0
    assert TK == TM and TM % ctx_len == 0
    assert seq % GRID_W == 0 and TM % (GRID_W * SUBLANES) == 0
    lat_tiles = lat_rows // TM
    all_tiles = lat_tiles + ctx_rows // TM
    tiles_per_batch = seq // TM
    group_of_tile = lambda t: jnp.minimum(t // tiles_per_batch, batch)

    c_rows = jnp.concatenate([c, c_ctx[None, :], jnp.zeros((SUBLANES - batch - 1, d), F32)], axis=0)
    mods_all = _modulation(c_rows, w_mod, b_mod)
    mods_all = jnp.transpose(mods_all[:, :, :batch + 1], (0, 2, 1, 3))

    head_cols = _head_block_order()
    rope_by_row, rope_by_col = _rope_tables(seq)
    head_mean = jnp.asarray(np.kron(np.eye(LANES // HEAD_DIM), np.full((HEAD_DIM, HEAD_DIM), 1.0 / HEAD_DIM)), BF16)

    xs = x.reshape(lat_rows, d)
    tail = ctx.reshape(ctx_rows, d)
    for i in range(depth):
        mods = mods_all[i]
        reads_ctx = i % 2 == 0
        later_reads_ctx = any(j % 2 == 0 for j in range(i + 1, depth))
        n1 = all_tiles if (reads_ctx or later_reads_ctx) else lat_tiles
        n2 = all_tiles if later_reads_ctx else lat_tiles
        common = dict(group_of_tile=group_of_tile)

        xs = _half_ffn(xs, mods, norm_g[i, 0][None], ffn_w13, ffn_w2, layer=i, half=0, n_tiles=n1,
                       tail=tail if i == 0 else None, **common)

        if reads_ctx:
            e = i // 2
            w_in = w_in_a[e]
            w_in = jnp.concatenate([w_in[:, :ATTN_WIDTH][:, head_cols], w_in[:, ATTN_WIDTH:]], axis=1).astype(BF16)
            w_out = jnp.concatenate([w_out_a[e][:ATTN_WIDTH][head_cols], w_out_a[e][ATTN_WIDTH:]], axis=0).astype(BF16)
            gq = jnp.tile(qk_norm_g[e, 0], LANES // HEAD_DIM)[None]
            gk = jnp.tile(qk_norm_g[e, 1], LANES // HEAD_DIM)[None]
            qt, k, vt, p = _project_a(xs, mods, norm_g[i, 1][None], w_in, gq, gk, head_mean,
                                      rope_by_row, rope_by_col, n_tiles=n1, lat_tiles=lat_tiles,
                                      tiles_per_batch=tiles_per_batch, **common)
            attn = _attention(qt, k, vt, batch=batch, seq=seq, ctx_len=ctx_len, latent=True)
            attn_c = (_attention(qt, k, vt, batch=batch, seq=seq, ctx_len=ctx_len, latent=False)
                      if later_reads_ctx else attn)
            xs = _combine_a(xs, mods, attn, attn_c, p, pool_w[e].astype(BF16), pool_scale[e][None], w_out,
                            n_tiles=n2, lat_rows=lat_rows, seq=seq, ctx_len=ctx_len, **common)
        else:
            o = i // 2
            xs = _mixer_c(xs, mods, norm_g[i, 1][None], w_in_c[o].astype(BF16), v_norm_g[o][None],
                          w_sp[o].astype(BF16), b_sp[o].T, w_out_c[o].astype(BF16), n_tiles=n2, **common)

        xs = _half_ffn(xs, mods, norm_g[i, 2][None], ffn_w13, ffn_w2, layer=i, half=1, n_tiles=n2, **common)
    return xs[:lat_rows].reshape(batch, seq, d)
```

```python
import functools

import jax
import jax.numpy as jnp
import numpy as np
from jax import lax
from jax.experimental import pallas as pl
from jax.experimental.pallas import tpu as pltpu

F32 = jnp.float32
BF16 = jnp.bfloat16

GRID_W = 64
HEAD_DIM = 64
N_Q_HEADS = 8
N_KV_HEADS = 2
Q_GROUP = N_Q_HEADS // N_KV_HEADS
ATTN_WIDTH = N_Q_HEADS * HEAD_DIM
KV_WIDTH = N_KV_HEADS * HEAD_DIM
ATTN_SCALE = HEAD_DIM ** -0.5
Q_SCALE_LOG2 = float(ATTN_SCALE * np.log2(np.e))
ROPE_THETA = 10000.0
POOL_WINDOWS = (2, 4, 8, 16)
POOL_GROUP_DIM = 128
CHUNK = 128
GMLP_GROUPS = 8
N_MOD = 9
EPS = 1e-6

LANES = 128
SUBLANES = 8
MXU_WIDTH = 256
VT_HEAD_ROWS = HEAD_DIM + 2 * SUBLANES
VT_ROWS = N_KV_HEADS * VT_HEAD_ROWS
VMEM_LIMIT_BYTES = 56 * 1024 * 1024

TM = 512
FF_CHUNK = 256
TQ = 256
TK = 512
HALO = 8


def _cparams(n_axes):
    return pltpu.CompilerParams(
        dimension_semantics=("arbitrary",) * n_axes,
        vmem_limit_bytes=VMEM_LIMIT_BYTES)


def _resident(shape):
    zeros = (0,) * len(shape)
    return pl.BlockSpec(shape, lambda *_: zeros, pipeline_mode=pl.Buffered(1))


def _modulate(x, g, shift, scale):
    ms = jnp.mean(x * x, axis=-1, keepdims=True)
    return x * lax.rsqrt(ms + EPS) * (g * (1.0 + scale)) + shift


MOD_PER_STEP = 3


def _mod_kernel(c_ref, w_ref, b_ref, o_ref):
    c = c_ref[...]
    d = c.shape[1]
    s = (c * jax.nn.sigmoid(c)).astype(BF16)
    m = jnp.dot(s, w_ref[0].astype(BF16), preferred_element_type=F32)
    for v in range(MOD_PER_STEP):
        o_ref[0, v] = m[:, v * d:(v + 1) * d] + b_ref[0, v]


def _modulation(c_rows, w_mod, b_mod):
    depth, d, _ = w_mod.shape
    rows = c_rows.shape[0]
    out = pl.pallas_call(
        _mod_kernel,
        grid=(depth, N_MOD // MOD_PER_STEP),
        in_specs=[
            pl.BlockSpec((rows, d), lambda i, j: (0, 0)),
            pl.BlockSpec((1, d, MOD_PER_STEP * d), lambda i, j: (i, 0, j)),
            pl.BlockSpec((1, MOD_PER_STEP, 1, d), lambda i, j: (i, j, 0, 0)),
        ],
        out_specs=pl.BlockSpec((1, MOD_PER_STEP, rows, d), lambda i, j: (i, j, 0, 0)),
        out_shape=jax.ShapeDtypeStruct((depth, N_MOD, rows, d), F32),
        compiler_params=_cparams(2),
        name="adaln_modulation",
    )(c_rows, w_mod, b_mod.reshape(depth, N_MOD, 1, d))
    return out


def _ffn_kernel(*refs, k0, head_tiles, layer, half):
    if head_tiles is None:
        x_ref, mod_ref, g_ref, w13_hbm, w2_hbm, o_ref, w13_ref, w2_ref, sem, h_ref, act_ref, acc_ref = refs
        x = x_ref[...]
    else:
        (x_ref, tail_ref, mod_ref, g_ref, w13_hbm, w2_hbm, o_ref, w13_ref, w2_ref, sem,
         h_ref, act_ref, acc_ref) = refs
        x = jnp.where(pl.program_id(0) < head_tiles, x_ref[...], tail_ref[...])
    mod = mod_ref[0]
    d_ff = w2_ref.shape[0]
    n_chunks = d_ff // FF_CHUNK
    first_step = pl.program_id(0) == 0

    def chunk(c, offset=0):
        start = c * FF_CHUNK + offset
        return pl.ds(start if isinstance(c, int) else pl.multiple_of(start, LANES), FF_CHUNK)

    def up_copy(c, part):
        cols = chunk(c, part * d_ff)
        return pltpu.make_async_copy(w13_hbm.at[layer, half, :, cols], w13_ref.at[:, cols], sem.at[part, c])

    def down_copy(c):
        return pltpu.make_async_copy(w2_hbm.at[layer, half, chunk(c), :], w2_ref.at[chunk(c), :], sem.at[2, c])

    def half_step(weights_in_flight):
        h_ref[...] = _modulate(x, g_ref[...], mod[k0:k0 + 1], mod[k0 + 1:k0 + 2]).astype(BF16)

        def gate(slot, c):
            if weights_in_flight:
                up_copy(c, 0).wait()
                up_copy(c, 1).wait()
            h = h_ref[...]
            a = jnp.dot(h, w13_ref[:, chunk(c)].astype(BF16), preferred_element_type=F32)
            b = jnp.dot(h, w13_ref[:, chunk(c, d_ff)].astype(BF16), preferred_element_type=F32)
            act_ref[slot] = (a * jax.nn.sigmoid(a) * b).astype(BF16)

        def down(slot, c, first=False):
            if weights_in_flight:
                down_copy(c).wait()
            part = jnp.dot(act_ref[slot], w2_ref[chunk(c), :].astype(BF16), preferred_element_type=F32)
            acc_ref[...] = part if first else acc_ref[...] + part

        assert n_chunks >= 2
        gate(0, 0)
        gate(1, 1)
        down(0, 0, first=True)

        def body(j, carry):
            c = 2 * j
            gate(0, c + 2)
            down(1, c + 1)
            gate(1, c + 3)
            down(0, c + 2)
            return carry

        lax.fori_loop(0, (n_chunks - 2) // 2, body, 0, unroll=True)
        if n_chunks % 2:
            gate(0, n_chunks - 1)
            down(1, n_chunks - 2)
            down(0, n_chunks - 1)
        else:
            down(1, n_chunks - 1)
        o_ref[...] = x + 0.5 * mod[k0 + 2:k0 + 3] * acc_ref[...]

    @pl.when(first_step)
    def _():
        for c in range(n_chunks):
            up_copy(c, 0).start()
            up_copy(c, 1).start()
            down_copy(c).start()
        half_step(weights_in_flight=True)

    @pl.when(jnp.logical_not(first_step))
    def _():
        half_step(weights_in_flight=False)


def _half_ffn(xs, mods, g, w13, w2, *, layer, half, n_tiles, group_of_tile, tail=None):
    d = xs.shape[1]
    d_ff = w2.shape[2]
    k0 = 6 * half
    in_hbm = pl.BlockSpec(memory_space=pl.ANY)
    if tail is None:
        head_tiles = None
        stream, stream_specs = [xs], [pl.BlockSpec((TM, d), lambda t: (t, 0))]
    else:
        head_tiles = xs.shape[0] // TM
        stream = [xs, tail]
        stream_specs = [pl.BlockSpec((TM, d), lambda t: (jnp.minimum(t, head_tiles - 1), 0)),
                        pl.BlockSpec((TM, d), lambda t: (jnp.maximum(t - head_tiles, 0), 0))]
    return pl.pallas_call(
        functools.partial(_ffn_kernel, k0=k0, head_tiles=head_tiles, layer=layer, half=half),
        grid=(n_tiles,),
        in_specs=stream_specs + [
            pl.BlockSpec((1, N_MOD, d), lambda t: (group_of_tile(t), 0, 0)),
            pl.BlockSpec((1, d), lambda t: (0, 0)),
            in_hbm,
            in_hbm,
        ],
        out_specs=pl.BlockSpec((TM, d), lambda t: (t, 0)),
        out_shape=jax.ShapeDtypeStruct((n_tiles * TM, d), F32),
        scratch_shapes=[pltpu.VMEM((d, 2 * d_ff), F32), pltpu.VMEM((d_ff, d), F32),
                        pltpu.SemaphoreType.DMA((3, d_ff // FF_CHUNK)),
                        pltpu.VMEM((TM, d), BF16), pltpu.VMEM((2, TM, FF_CHUNK), BF16),
                        pltpu.VMEM((TM, d), F32)],
        compiler_params=_cparams(1),
        name="half_ffn",
    )(*stream, mods, g, w13, w2)


def _rope(x, cos, sin_hi, sin_lo):
    return (x * cos + pltpu.roll(x, 16, axis=1) * sin_hi
            + pltpu.roll(x, LANES - 16, axis=1) * sin_lo)


def _proj_a_kernel(x_ref, mod_ref, g_ref, w_ref, gq_ref, gk_ref, hsum_ref,
                   rope_row_ref, rope_col_ref, qt_ref, k_ref, vt_ref, p_ref, *, lat_tiles):
    mod = mod_ref[0]
    h = _modulate(x_ref[...], g_ref[...], mod[3:4], mod[4:5]).astype(BF16)
    z = jnp.dot(h, w_ref[...], preferred_element_type=F32)

    by_row, by_col = rope_row_ref[...], rope_col_ref[...]
    rope = jnp.concatenate([jnp.broadcast_to(by_row[r:r + 1, :], by_col.shape) + by_col
                            for r in range(TM // GRID_W)], axis=0)
    is_ctx = pl.program_id(0) >= lat_tiles
    cos = jnp.where(is_ctx, 1.0, rope[:, :LANES])
    shi = jnp.where(is_ctx, 0.0, rope[:, LANES:2 * LANES])
    slo = jnp.where(is_ctx, 0.0, rope[:, 2 * LANES:])
    hsum = hsum_ref[...]

    def norm_rope(zb, gain):
        ms = jnp.dot((zb * zb).astype(BF16), hsum, preferred_element_type=F32)
        return _rope(zb * lax.rsqrt(ms + EPS) * gain, cos, shi, slo)

    for j in range(ATTN_WIDTH // LANES):
        qb = norm_rope(z[:, j * LANES:(j + 1) * LANES], gq_ref[...])
        qt_ref[j * LANES:(j + 1) * LANES, :] = (qb * Q_SCALE_LOG2).T.astype(BF16)
    k_ref[...] = norm_rope(z[:, ATTN_WIDTH:ATTN_WIDTH + KV_WIDTH], gk_ref[...]).astype(BF16)
    v_t = z[:, ATTN_WIDTH + KV_WIDTH:ATTN_WIDTH + 2 * KV_WIDTH].T.astype(BF16)
    ones_row = (lax.broadcasted_iota(jnp.int32, (VT_HEAD_ROWS - HEAD_DIM, TM), 0) == 0).astype(BF16)
    vt_ref[0] = jnp.concatenate([v_t[:HEAD_DIM], ones_row, v_t[HEAD_DIM:], ones_row], axis=0)
    p_ref[...] = z[:, ATTN_WIDTH + 2 * KV_WIDTH:]


def _project_a(xs, mods, g, w_in, gq, gk, hsum, rope_by_row, rope_by_col, *, n_tiles, group_of_tile,
               lat_tiles, tiles_per_batch):
    d = xs.shape[1]
    nt = n_tiles * TM
    grid_rows = TM // GRID_W
    pool_width = w_in.shape[1] - ATTN_WIDTH - 2 * KV_WIDTH
    row_tile = lambda w: pl.BlockSpec((TM, w), lambda t: (t, 0))
    const2 = lambda shape: pl.BlockSpec(shape, lambda t: (0, 0))
    return pl.pallas_call(
        functools.partial(_proj_a_kernel, lat_tiles=lat_tiles),
        grid=(n_tiles,),
        in_specs=[
            row_tile(d),
            pl.BlockSpec((1, N_MOD, d), lambda t: (group_of_tile(t), 0, 0)),
            const2((1, d)),
            _resident(w_in.shape),
            const2((1, LANES)), const2((1, LANES)), const2((LANES, LANES)),
            pl.BlockSpec((grid_rows, 3 * LANES), lambda t: (t % tiles_per_batch, 0)),
            const2(rope_by_col.shape),
        ],
        out_specs=[
            pl.BlockSpec((ATTN_WIDTH, TM), lambda t: (0, t)),
            row_tile(KV_WIDTH),
            pl.BlockSpec((1, VT_ROWS, TM), lambda t: (t, 0, 0)),
            row_tile(pool_width),
        ],
        out_shape=[
            jax.ShapeDtypeStruct((ATTN_WIDTH, nt), BF16),
            jax.ShapeDtypeStruct((nt, KV_WIDTH), BF16),
            jax.ShapeDtypeStruct((n_tiles, VT_ROWS, TM), BF16),
            jax.ShapeDtypeStruct((nt, pool_width), F32),
        ],
        compiler_params=_cparams(1),
        name="mixer_a_project",
    )(xs, mods, g, w_in, gq, gk, hsum, rope_by_row, rope_by_col)


def _attn_kernel(*refs, n_lat_steps):
    if n_lat_steps:
        qt_ref, kc_ref, vtc_ref, kl_ref, vtl_ref, o_ref, m_ref, acc_ref, s_ref = refs
    else:
        qt_ref, kc_ref, vtc_ref, o_ref, m_ref, acc_ref = refs
    tq = qt_ref.shape[1]
    n_blocks = ATTN_WIDTH // LANES
    top = lax.broadcasted_iota(jnp.int32, (LANES, tq), 0) < HEAD_DIM
    pieces = []
    for j in range(n_blocks):
        blk = qt_ref[j * LANES:(j + 1) * LANES, :]
        zero = jnp.zeros_like(blk)
        pieces += [jnp.where(top, blk, zero), jnp.where(top, zero, blk)]
    qqt = jnp.concatenate(pieces, axis=1)

    m_ref[...] = jnp.full_like(m_ref, -jnp.inf)
    acc_ref[...] = jnp.zeros_like(acc_ref)

    assert tq % MXU_WIDTH == 0
    col_tiles = [(slice(ct * MXU_WIDTH, (ct + 1) * MXU_WIDTH), (ct * MXU_WIDTH // tq) % N_KV_HEADS)
                 for ct in range(qqt.shape[1] // MXU_WIDTH)]

    def accumulate(s, vt, cs, kv):
        m_old = m_ref[:, cs]
        m_new = jnp.maximum(m_old, jnp.max(s, axis=0, keepdims=True))
        alpha = jnp.exp2(m_old - m_new)
        p = jnp.exp2(s - m_new).astype(BF16)
        pv = jnp.dot(vt[kv * VT_HEAD_ROWS:(kv + 1) * VT_HEAD_ROWS, :], p, preferred_element_type=F32)
        acc_ref[:, cs] = alpha * acc_ref[:, cs] + pv
        m_ref[:, cs] = m_new

    def lat_keys(i):
        return kl_ref[pl.ds(pl.multiple_of(i * TK, TK), TK), :]

    def step(nxt, cur):
        for cs, kv in col_tiles:
            if nxt is not None:
                s_ref[nxt[0], :, cs] = jnp.dot(nxt[1], qqt[:, cs], preferred_element_type=F32)
            if cur is not None:
                accumulate(cur[0][:, cs], cur[1], cs, kv)

    s_ctx = jnp.dot(kc_ref[...], qqt, preferred_element_type=F32)
    if n_lat_steps:
        step((1, lat_keys(0)), (s_ctx, vtc_ref[0]))

        def body(j, carry):
            i = 2 * j
            step((0, lat_keys(i + 1)), (s_ref.at[1], vtl_ref[i]))
            step((1, lat_keys(i + 2)), (s_ref.at[0], vtl_ref[i + 1]))
            return carry
        lax.fori_loop(0, n_lat_steps // 2 - 1, body, 0, unroll=3)
        step((0, lat_keys(n_lat_steps - 1)), (s_ref.at[1], vtl_ref[n_lat_steps - 2]))
        step(None, (s_ref.at[0], vtl_ref[n_lat_steps - 1]))
    else:
        step(None, (s_ctx, vtc_ref[0]))

    out_t = acc_ref[:HEAD_DIM, :] / acc_ref[HEAD_DIM:HEAD_DIM + 1, :]
    for j in range(n_blocks):
        blk = jnp.concatenate([out_t[:, (2 * j) * tq:(2 * j + 1) * tq],
                               out_t[:, (2 * j + 1) * tq:(2 * j + 2) * tq]], axis=0)
        o_ref[:, j * LANES:(j + 1) * LANES] = blk.T.astype(o_ref.dtype)


def _attention(qt, k, vt, *, batch, seq, ctx_len, latent):
    lat_rows = batch * seq
    n_q = (seq if latent else ctx_len) // TQ
    q_blk0 = 0 if latent else lat_rows // TQ
    ctx_per_slab = TM // ctx_len
    ctx_slab0 = lat_rows // TM
    in_specs = [
        pl.BlockSpec((ATTN_WIDTH, TQ), lambda b, i: (0, q_blk0 + b * n_q + i)),
        pl.BlockSpec((ctx_len, KV_WIDTH), lambda b, i: (lat_rows // ctx_len + b, 0)),
        pl.BlockSpec((1, VT_ROWS, ctx_len),
                     lambda b, i: (ctx_slab0 + b // ctx_per_slab, 0, b % ctx_per_slab)),
    ]
    operands = [qt, k, vt]
    if latent:
        in_specs += [pl.BlockSpec((seq, KV_WIDTH), lambda b, i: (b, 0)),
                     pl.BlockSpec((seq // TM, VT_ROWS, TM), lambda b, i: (b, 0, 0))]
        operands += [k, vt]
    cols = N_Q_HEADS * TQ
    scratch = [pltpu.VMEM((1, cols), F32), pltpu.VMEM((VT_HEAD_ROWS, cols), F32)]
    if latent:
        assert (seq // TK) % 2 == 0 and seq // TK >= 2
        scratch.append(pltpu.VMEM((2, TK, cols), F32))
    return pl.pallas_call(
        functools.partial(_attn_kernel, n_lat_steps=seq // TK if latent else 0),
        grid=(batch, n_q),
        in_specs=in_specs,
        out_specs=pl.BlockSpec((TQ, ATTN_WIDTH), lambda b, i: (b * n_q + i, 0)),
        out_shape=jax.ShapeDtypeStruct((batch * n_q * TQ, ATTN_WIDTH), BF16),
        scratch_shapes=scratch,
        compiler_params=_cparams(2),
        name="gqa_attention_latent" if latent else "gqa_attention_ctx",
    )(*operands)


def _shift_rows(a, d):
    return pltpu.roll(a, d % a.shape[0], axis=0)


def _pool_group(pe, w, edge):
    def before(a, span):
        shifted = _shift_rows(a, span)
        return shifted if edge is None else jnp.where(edge[0] >= span, shifted, jnp.zeros_like(a))

    def after(a, span):
        shifted = _shift_rows(a, -span)
        return shifted if edge is None else jnp.where(edge[0] + span < edge[1], shifted, jnp.zeros_like(a))

    left = before(pe, 1)
    right = pe
    span = 1
    while 2 * span <= w // 2:
        left = left + before(left, span)
        right = right + after(right, span)
        span *= 2
    if edge is None:
        return (left + right) * (1.0 / w)
    pos, n = edge
    count = jnp.maximum(jnp.minimum(pos + w // 2, n) - jnp.maximum(pos - w // 2, 0), 1)
    return (left + right) / count.astype(F32)


def _rem_static(r, m):
    return r & (m - 1) if m & (m - 1) == 0 else lax.rem(r, m)


def _comb_a_kernel(x_ref, mod_ref, attn_ref, attn_tail_ref, p_ref, pprev_ref, pnext_ref, pw_ref, ps_ref,
                   wo_ref, o_ref, pooled_ref, *, lat_rows, seq, ctx_len, attn_head_tiles):
    tile = pl.program_id(0)
    tiles_per_seq = seq // TM

    def mix(edge):
        attn = jnp.where(tile < attn_head_tiles, attn_ref[...], attn_tail_ref[...])
        mixed = jnp.dot(attn, wo_ref[:ATTN_WIDTH, :], preferred_element_type=F32)
        for gi, w in enumerate(POOL_WINDOWS):
            cols = slice(gi * POOL_GROUP_DIM, (gi + 1) * POOL_GROUP_DIM)
            pe = jnp.concatenate([pprev_ref[:, cols], p_ref[:, cols], pnext_ref[:, cols]], axis=0)
            centred = (_pool_group(pe, w, edge) - pe)[HALO:HALO + TM]
            pooled = jnp.dot(centred.astype(BF16), pw_ref[gi], preferred_element_type=F32) * ps_ref[:, cols]
            pooled_ref[:, cols] = pooled.astype(BF16)
        mixed += jnp.dot(pooled_ref[...], wo_ref[ATTN_WIDTH:, :], preferred_element_type=F32)
        o_ref[...] = x_ref[...] + mod_ref[0][5:6] * mixed

    seq_tile = lax.rem(tile, tiles_per_seq)
    at_edge = (tile >= lat_rows // TM) | (seq_tile == 0) | (seq_tile == tiles_per_seq - 1)

    @pl.when(at_edge)
    def _():
        rows = TM + 2 * HALO
        r = tile * TM - HALO + lax.broadcasted_iota(jnp.int32, (rows, POOL_GROUP_DIM), 0)
        is_lat = r < lat_rows
        n = jnp.where(is_lat, seq, ctx_len)
        pos = jnp.where(is_lat, _rem_static(r, seq), _rem_static(r - lat_rows, ctx_len))
        mix((pos, n))

    @pl.when(jnp.logical_not(at_edge))
    def _():
        mix(None)


def _combine_a(xs, mods, attn, attn_tail, p, pool_w, pool_scale, w_out, *, n_tiles, group_of_tile,
               lat_rows, seq, ctx_len):
    d = xs.shape[1]
    pool_width = p.shape[1]
    halo_per_tile = TM // HALO
    last_halo_blk = p.shape[0] // HALO - 1
    head_tiles = attn.shape[0] // TM
    return pl.pallas_call(
        functools.partial(_comb_a_kernel, lat_rows=lat_rows, seq=seq, ctx_len=ctx_len,
                          attn_head_tiles=head_tiles),
        grid=(n_tiles,),
        in_specs=[
            pl.BlockSpec((TM, d), lambda t: (t, 0)),
            pl.BlockSpec((1, N_MOD, d), lambda t: (group_of_tile(t), 0, 0)),
            pl.BlockSpec((TM, ATTN_WIDTH), lambda t: (jnp.minimum(t, head_tiles - 1), 0)),
            pl.BlockSpec((TM, ATTN_WIDTH), lambda t: (jnp.maximum(t - head_tiles, 0), 0)),
            pl.BlockSpec((TM, pool_width), lambda t: (t, 0)),
            pl.BlockSpec((HALO, pool_width), lambda t: (jnp.maximum(t * halo_per_tile - 1, 0), 0)),
            pl.BlockSpec((HALO, pool_width),
                         lambda t: (jnp.minimum((t + 1) * halo_per_tile, last_halo_blk), 0)),
            _resident(pool_w.shape),
            pl.BlockSpec((1, pool_width), lambda t: (0, 0)),
            _resident(w_out.shape),
        ],
        out_specs=pl.BlockSpec((TM, d), lambda t: (t, 0)),
        out_shape=jax.ShapeDtypeStruct((n_tiles * TM, d), F32),
        scratch_shapes=[pltpu.VMEM((TM, pool_width), BF16)],
        compiler_params=_cparams(1),
        name="mixer_a_combine",
    )(xs, mods, attn, attn_tail, p, p, p, pool_w, pool_scale, w_out)


def _gelu_tanh(x):
    return 0.5 * x * (1.0 + jnp.tanh(np.sqrt(2.0 / np.pi).astype(np.float32) * (x + 0.044715 * (x * x * x))))


def _mixer_c_kernel(x_ref, mod_ref, g_ref, wi_ref, vg_ref, wsp_ref, bsp_ref, wo_ref, o_ref, gated_ref):
    x = x_ref[...]
    mod = mod_ref[0]
    width = wo_ref.shape[0]
    h = _modulate(x, g_ref[...], mod[3:4], mod[4:5]).astype(BF16)
    z = _gelu_tanh(jnp.dot(h, wi_ref[...], preferred_element_type=F32))
    u = z[:, :width]
    v = z[:, width:]
    v = (v * lax.rsqrt(jnp.mean(v * v, axis=-1, keepdims=True) + EPS) * vg_ref[...]).astype(BF16)
    bias = bsp_ref[...]
    for g in range(GMLP_GROUPS):
        cols = slice(g * CHUNK, (g + 1) * CHUNK)
        b_col = bias[:, g:g + 1]
        for c in range(TM // CHUNK):
            rows = slice(c * CHUNK, (c + 1) * CHUNK)
            sv = jnp.dot(wsp_ref[g], v[rows, cols], preferred_element_type=F32) + b_col
            gated_ref[rows, cols] = (u[rows, cols] * sv).astype(BF16)
    o_ref[...] = x + mod[5:6] * jnp.dot(gated_ref[...], wo_ref[...], preferred_element_type=F32)


def _mixer_c(xs, mods, g, w_in, v_g, w_sp, b_sp_cols, w_out, *, n_tiles, group_of_tile):
    d = xs.shape[1]
    width = w_out.shape[0]
    return pl.pallas_call(
        _mixer_c_kernel,
        grid=(n_tiles,),
        in_specs=[
            pl.BlockSpec((TM, d), lambda t: (t, 0)),
            pl.BlockSpec((1, N_MOD, d), lambda t: (group_of_tile(t), 0, 0)),
            pl.BlockSpec((1, d), lambda t: (0, 0)),
            _resident(w_in.shape),
            pl.BlockSpec((1, width), lambda t: (0, 0)),
            _resident(w_sp.shape),
            pl.BlockSpec(b_sp_cols.shape, lambda t: (0, 0)),
            _resident(w_out.shape),
        ],
        out_specs=pl.BlockSpec((TM, d), lambda t: (t, 0)),
        out_shape=jax.ShapeDtypeStruct((n_tiles * TM, d), F32),
        scratch_shapes=[pltpu.VMEM((TM, width), BF16)],
        compiler_params=_cparams(1),
        name="mixer_c_gmlp",
    )(xs, mods, g, w_in, v_g, w_sp, b_sp_cols, w_out)


def _rope_tables(seq):
    half = HEAD_DIM // 2
    inv_freq = ROPE_THETA ** (-jnp.arange(0, half, 2, dtype=F32) / half)

    def parts(n, row_axis):
        ang = jnp.arange(n).astype(F32)[:, None] * inv_freq[None, :]
        cos, sin, zero = jnp.cos(ang), jnp.sin(ang), jnp.zeros_like(ang)
        if row_axis:
            heads = ([cos, cos, zero, zero], [zero, sin, zero, zero], [-sin, zero, zero, zero])
        else:
            heads = ([zero, zero, cos, cos], [zero, zero, zero, sin], [zero, zero, -sin, zero])
        return jnp.concatenate([blk for head in heads for blk in head * (LANES // HEAD_DIM)], axis=1)

    return parts(seq // GRID_W, True), parts(GRID_W, False)


def _head_block_order():
    order = []
    for j in range(Q_GROUP):
        order += [j, j + Q_GROUP]
    return np.concatenate([np.arange(h * HEAD_DIM, (h + 1) * HEAD_DIM) for h in order])


def kernel(x, c, ctx, c_ctx, w_mod, b_mod, norm_g, ffn_w13, ffn_w2, w_in_a, qk_norm_g, pool_w,
           pool_scale, w_out_a, w_in_c, v_norm_g, w_sp, b_sp, w_out_c):
    batch, seq, d = x.shape
    ctx_len = ctx.shape[1]
    depth = w_mod.shape[0]
    d_ff = ffn_w2.shape[2]
    lat_rows = batch * seq
    ctx_rows = batch * ctx_len
    assert seq % TM == 0 and ctx_rows % TM == 0 and seq % TK == 0 and ctx_len % TQ == 0
    assert ctx_len % HALO == 0 and lat_rows % ctx_len == 0 and d_ff % FF_CHUNK == 0
    assert TK == TM and TM % ctx_len == 0
    assert seq % GRID_W == 0 and TM % (GRID_W * SUBLANES) == 0
    lat_tiles = lat_rows // TM
    all_tiles = lat_tiles + ctx_rows // TM
    tiles_per_batch = seq // TM
    group_of_tile = lambda t: jnp.minimum(t // tiles_per_batch, batch)

    c_rows = jnp.concatenate([c, c_ctx[None, :], jnp.zeros((SUBLANES - batch - 1, d), F32)], axis=0)
    mods_all = _modulation(c_rows, w_mod, b_mod)
    mods_all = jnp.transpose(mods_all[:, :, :batch + 1], (0, 2, 1, 3))

    head_cols = _head_block_order()
    rope_by_row, rope_by_col = _rope_tables(seq)
    head_mean = jnp.asarray(np.kron(np.eye(LANES // HEAD_DIM), np.full((HEAD_DIM, HEAD_DIM), 1.0 / HEAD_DIM)), BF16)

    xs = x.reshape(lat_rows, d)
    tail = ctx.reshape(ctx_rows, d)
    for i in range(depth):
        mods = mods_all[i]
        reads_ctx = i % 2 == 0
        later_reads_ctx = any(j % 2 == 0 for j in range(i + 1, depth))
        n1 = all_tiles if (reads_ctx or later_reads_ctx) else lat_tiles
        n2 = all_tiles if later_reads_ctx else lat_tiles
        common = dict(group_of_tile=group_of_tile)

        xs = _half_ffn(xs, mods, norm_g[i, 0][None], ffn_w13, ffn_w2, layer=i, half=0, n_tiles=n1,
                       tail=tail if i == 0 else None, **common)

        if reads_ctx:
            e = i // 2
            w_in = w_in_a[e]
            w_in = jnp.concatenate([w_in[:, :ATTN_WIDTH][:, head_cols], w_in[:, ATTN_WIDTH:]], axis=1).astype(BF16)
            w_out = jnp.concatenate([w_out_a[e][:ATTN_WIDTH][head_cols], w_out_a[e][ATTN_WIDTH:]], axis=0).astype(BF16)
            gq = jnp.tile(qk_norm_g[e, 0], LANES // HEAD_DIM)[None]
            gk = jnp.tile(qk_norm_g[e, 1], LANES // HEAD_DIM)[None]
            qt, k, vt, p = _project_a(xs, mods, norm_g[i, 1][None], w_in, gq, gk, head_mean,
                                      rope_by_row, rope_by_col, n_tiles=n1, lat_tiles=lat_tiles,
                                      tiles_per_batch=tiles_per_batch, **common)
            attn = _attention(qt, k, vt, batch=batch, seq=seq, ctx_len=ctx_len, latent=True)
            attn_c = (_attention(qt, k, vt, batch=batch, seq=seq, ctx_len=ctx_len, latent=False)
                      if later_reads_ctx else attn)
            xs = _combine_a(xs, mods, attn, attn_c, p, pool_w[e].astype(BF16), pool_scale[e][None], w_out,
                            n_tiles=n2, lat_rows=lat_rows, seq=seq, ctx_len=ctx_len, **common)
        else:
            o = i // 2
            xs = _mixer_c(xs, mods, norm_g[i, 1][None], w_in_c[o].astype(BF16), v_norm_g[o][None],
                          w_sp[o].astype(BF16), b_sp[o].T, w_out_c[o].astype(BF16), n_tiles=n2, **common)

        xs = _half_ffn(xs, mods, norm_g[i, 2][None], ffn_w13, ffn_w2, layer=i, half=1, n_tiles=n2, **common)
    return xs[:lat_rows].reshape(batch, seq, d)
```
